```python
import jax, jax.numpy as jnp
from jax import lax
import numpy as np

D_MODEL = 4096
BATCH = 4
SEQ = 4096
DEPTH = 2
DEC_BATCH = 16
DEC_SEQ = 32
PAST_LEN = 2048

CHUNK = 64
MIX_DIM = D_MODEL
RWKV_HEAD = 64
RWKV_DIM = D_MODEL // 4
RWKV_HEADS = RWKV_DIM // RWKV_HEAD
RWKV_LORA = 64
RWKV_SHIFT_DIM = 3 * RWKV_DIM + 2 * RWKV_LORA
RWKV_GN_EPS = 64e-5
MLA_HEADS = 16
MLA_NOPE = 128
MLA_ROPE = 64
MLA_VHEAD = 128
MLA_DIM = MLA_HEADS * MLA_VHEAD
Q_LORA = D_MODEL // 4
KV_LORA = 512
ROPE_BASE = 10000.0
Q_BLOCK = 128
CONV_DIM = MIX_DIM - RWKV_DIM - MLA_DIM
CONV_W = 3
NORM_EPS = 1e-6
IN_COLS = (RWKV_SHIFT_DIM + RWKV_DIM + Q_LORA + KV_LORA + MLA_ROPE + MLA_DIM
           + 4 * CONV_DIM)

kernel_name = "hybrid_rwkv7_mla_shortconv_stream_step"


def rms_norm(x, g, eps=NORM_EPS):
    xf = x.astype(jnp.float32)
    y = xf * lax.rsqrt(jnp.mean(xf * xf, axis=-1, keepdims=True) + eps)
    return (y * g.astype(jnp.float32)).astype(x.dtype)


def rotary(x, pos):
    half = MLA_ROPE // 2
    freqs = ROPE_BASE ** (-jnp.arange(half, dtype=jnp.float32) / half)
    ang = pos.astype(jnp.float32)[:, None] * freqs[None, :]
    ang = ang.reshape((ang.shape[0],) + (1,) * (x.ndim - 3) + (half,))
    cos, sin = jnp.cos(ang), jnp.sin(ang)
    xf = x.astype(jnp.float32)
    x1, x2 = xf[..., :half], xf[..., half:]
    return jnp.concatenate([x1 * cos - x2 * sin, x2 * cos + x1 * sin], axis=-1).astype(x.dtype)


def split_columns(p):
    sizes = (RWKV_SHIFT_DIM, RWKV_DIM, Q_LORA, KV_LORA, MLA_ROPE, MLA_DIM,
             CONV_DIM, CONV_DIM, CONV_DIM, CONV_DIM)
    out, start = [], 0
    for s in sizes:
        out.append(p[..., start:start + s])
        start += s
    return out


def block_causal_attention(q_nope, q_rope, k_nope, k_rope, v, q_pos, k_pos):
    scale = (MLA_NOPE + MLA_ROPE) ** -0.5
    k_chunk = k_pos // CHUNK
    q_chunk = q_pos // CHUNK

    def one_block(args):
        qn, qr, qc = args
        s = (jnp.einsum('bqhd,bkhd->bhqk', qn, k_nope).astype(jnp.float32)
             + jnp.einsum('bqhd,bkd->bhqk', qr, k_rope).astype(jnp.float32)) * scale
        visible = k_chunk[None, :] <= qc[:, None]
        s = jnp.where(visible[None, None], s, -1e30)
        p = jax.nn.softmax(s, axis=-1)
        return jnp.einsum('bhqk,bkhd->bqhd', p.astype(v.dtype), v)

    B, T = q_nope.shape[:2]
    if T > Q_BLOCK and T % Q_BLOCK == 0:
        nb = T // Q_BLOCK

        def blocks(t):
            return jnp.moveaxis(t.reshape((B, nb, Q_BLOCK) + t.shape[2:]), 1, 0)

        out = lax.map(one_block, (blocks(q_nope), blocks(q_rope), q_chunk.reshape(nb, Q_BLOCK)))
        return jnp.moveaxis(out, 0, 1).reshape((B, T) + out.shape[3:])
    return one_block((q_nope, q_rope, q_chunk))


def rwkv7_scan(S0, r, w, k, v, kk, a):
    def step(S, inp):
        r_t, w_t, k_t, v_t, kk_t, a_t = inp
        sa = jnp.einsum('bhvk,bhk->bhv', S, -kk_t)
        S = (S * w_t[:, :, None, :] + sa[..., :, None] * (kk_t * a_t)[..., None, :]
             + v_t[..., :, None] * k_t[..., None, :])
        o = jnp.einsum('bhvk,bhk->bhv', S, r_t)
        return S, o

    xs = tuple(jnp.moveaxis(t, 1, 0) for t in (r, w, k, v, kk, a))
    S, o = lax.scan(step, S0, xs)
    return S, jnp.moveaxis(o, 0, 1)


def mixer_layer(x, c, rw_state, rw_shift, conv_buf, lat_past, kr_past,
                w_ada, b_ada, norm_g, w_in, rw_mu, rw_w0, rw_w2, rw_a0, rw_a2,
                rw_kk, rw_ka, rw_rk, rw_ln_g, rw_ln_b, mla_qnorm_g, mla_kvnorm_g,
                mla_w_uq, mla_w_uk, mla_w_uv, mla_qn_nope, mla_qn_rope,
                mla_kn_nope, mla_kn_rope, conv_w, conv_b, w_out):
    B, T, _ = x.shape
    P = lat_past.shape[1]
    f32 = jnp.float32
    q_pos = jnp.arange(P, P + T, dtype=jnp.int32)
    k_pos = jnp.arange(P + T, dtype=jnp.int32)

    mod = c @ w_ada + b_ada
    shift, scale, gate = jnp.split(mod, 3, axis=-1)
    h = rms_norm(x, norm_g) * (1 + scale[:, None]) + shift[:, None]
    proj = h @ w_in
    (rw_pre, rw_gate, cq, ckv, kr_raw, mla_gate,
     cv_b, cv_c, cv_x, cv_gate) = split_columns(proj)

    prev = jnp.concatenate([rw_shift[:, None].astype(rw_pre.dtype), rw_pre[:, :-1]], axis=1)
    xs = rw_pre + (prev - rw_pre) * rw_mu
    r = xs[..., :RWKV_DIM]
    k = xs[..., RWKV_DIM:2 * RWKV_DIM]
    v = xs[..., 2 * RWKV_DIM:3 * RWKV_DIM]
    wl = xs[..., 3 * RWKV_DIM:3 * RWKV_DIM + RWKV_LORA]
    al = xs[..., 3 * RWKV_DIM + RWKV_LORA:]
    w_log = -jax.nn.softplus(-(rw_w0 + jnp.tanh(wl) @ rw_w2).astype(f32)) - 0.5
    decay = jnp.exp(-jnp.exp(w_log))
    a = jax.nn.sigmoid((rw_a0 + al @ rw_a2).astype(f32))

    def heads(t):
        return t.astype(f32).reshape(B, T, RWKV_HEADS, RWKV_HEAD)

    kk = heads(k.astype(f32) * rw_kk.astype(f32))
    kk = kk * lax.rsqrt(jnp.sum(kk * kk, axis=-1, keepdims=True) + 1e-12)
    k_mod = k.astype(f32) * (1 + (a - 1) * rw_ka.astype(f32))
    rh, kh, vh, ah, dh = heads(r), heads(k_mod), heads(v), heads(a), heads(decay)
    new_rw_state, o = rwkv7_scan(rw_state.astype(f32), rh, dh, kh, vh, kk, ah)
    mu = jnp.mean(o, axis=-1, keepdims=True)
    var = jnp.mean(jnp.square(o - mu), axis=-1, keepdims=True)
    o = ((o - mu) * lax.rsqrt(var + RWKV_GN_EPS)).reshape(B, T, RWKV_DIM)
    o = o * rw_ln_g.astype(f32) + rw_ln_b.astype(f32)
    bonus = jnp.sum(rh * kh * rw_rk.astype(f32), axis=-1, keepdims=True) * vh
    rw_out = (o + bonus.reshape(B, T, RWKV_DIM)).astype(x.dtype) * jax.nn.silu(rw_gate)

    q = (rms_norm(cq, mla_qnorm_g) @ mla_w_uq).reshape(B, T, MLA_HEADS, MLA_NOPE + MLA_ROPE)
    q_nope = rms_norm(q[..., :MLA_NOPE], mla_qn_nope)
    q_rope = rotary(rms_norm(q[..., MLA_NOPE:], mla_qn_rope), q_pos)
    lat_new = rms_norm(ckv, mla_kvnorm_g)
    kr_new = rotary(rms_norm(kr_raw, mla_kn_rope), q_pos)
    lat_all = jnp.concatenate([lat_past.astype(lat_new.dtype), lat_new], axis=1)
    kr_all = jnp.concatenate([kr_past.astype(kr_new.dtype), kr_new], axis=1)
    k_nope = rms_norm((lat_all @ mla_w_uk).reshape(B, P + T, MLA_HEADS, MLA_NOPE), mla_kn_nope)
    v_mla = (lat_all @ mla_w_uv).reshape(B, P + T, MLA_HEADS, MLA_VHEAD)
    attn = block_causal_attention(q_nope, q_rope, k_nope, kr_all, v_mla, q_pos, k_pos)
    mla_out = attn.reshape(B, T, MLA_DIM) * jax.nn.silu(mla_gate)

    u = cv_c * cv_x
    up = jnp.concatenate([conv_buf.astype(u.dtype), u], axis=1)
    y = conv_b + up[:, 0:T] * conv_w[0]
    for j in range(1, CONV_W):
        y = y + up[:, j:j + T] * conv_w[j]
    cv_out = cv_b * y * jax.nn.silu(cv_gate)
    new_conv = up[:, T:]

    mix = jnp.concatenate([rw_out, mla_out, cv_out], axis=-1) @ w_out
    x_out = x + gate[:, None] * mix
    return x_out, lat_new, kr_new, new_rw_state.astype(x.dtype), rw_pre[:, -1], new_conv


def setup_inputs(seed: int = 0) -> dict:
    key = jax.random.key(seed)
    ks = iter(jax.random.split(key, 48))

    def nrm(shape, s):
        return s * jax.random.normal(next(ks), shape, jnp.float32)

    D = D_MODEL
    return {
        "x_prompt": nrm((BATCH, SEQ, D), 1.0),
        "x_sample": nrm((DEC_BATCH, DEC_SEQ, D), 1.0),
        "c_prompt": nrm((BATCH, D), 1.0),
        "c_sample": nrm((DEC_BATCH, D), 1.0),
        "cache_mla_latent": nrm((DEPTH, DEC_BATCH, PAST_LEN, KV_LORA), 1.0),
        "cache_mla_krope": nrm((DEPTH, DEC_BATCH, PAST_LEN, MLA_ROPE), 1.0),
        "state_rwkv": nrm((DEPTH, DEC_BATCH, RWKV_HEADS, RWKV_HEAD, RWKV_HEAD), 0.3),
        "state_rwkv_shift": nrm((DEPTH, DEC_BATCH, RWKV_SHIFT_DIM), 1.0),
        "state_conv": nrm((DEPTH, DEC_BATCH, CONV_W - 1, CONV_DIM), 1.0),
        "w_ada": nrm((DEPTH, D, 3 * D), 0.3 * D ** -0.5),
        "b_ada": nrm((DEPTH, 3 * D), 0.02),
        "norm_g": 1.0 + nrm((DEPTH, D), 0.02),
        "w_in": nrm((DEPTH, D, IN_COLS), D ** -0.5),
        "rw_mu": jax.random.uniform(next(ks), (DEPTH, RWKV_SHIFT_DIM), jnp.float32),
        "rw_w0": nrm((DEPTH, RWKV_DIM), 0.5),
        "rw_w2": nrm((DEPTH, RWKV_LORA, RWKV_DIM), 0.5 * RWKV_LORA ** -0.5),
        "rw_a0": nrm((DEPTH, RWKV_DIM), 0.5),
        "rw_a2": nrm((DEPTH, RWKV_LORA, RWKV_DIM), 0.5 * RWKV_LORA ** -0.5),
        "rw_kk": 0.85 + nrm((DEPTH, RWKV_DIM), 0.02),
        "rw_ka": 1.0 + nrm((DEPTH, RWKV_DIM), 0.02),
        "rw_rk": nrm((DEPTH, RWKV_HEADS, RWKV_HEAD), 0.1),
        "rw_ln_g": 1.0 + nrm((DEPTH, RWKV_DIM), 0.02),
        "rw_ln_b": nrm((DEPTH, RWKV_DIM), 0.02),
        "mla_qnorm_g": 1.0 + nrm((DEPTH, Q_LORA), 0.02),
        "mla_kvnorm_g": 1.0 + nrm((DEPTH, KV_LORA), 0.02),
        "mla_w_uq": nrm((DEPTH, Q_LORA, MLA_HEADS * (MLA_NOPE + MLA_ROPE)), Q_LORA ** -0.5),
        "mla_w_uk": nrm((DEPTH, KV_LORA, MLA_HEADS * MLA_NOPE), KV_LORA ** -0.5),
        "mla_w_uv": nrm((DEPTH, KV_LORA, MLA_HEADS * MLA_VHEAD), KV_LORA ** -0.5),
        "mla_qn_nope": 1.0 + nrm((DEPTH, MLA_NOPE), 0.02),
        "mla_qn_rope": 1.0 + nrm((DEPTH, MLA_ROPE), 0.02),
        "mla_kn_nope": 1.0 + nrm((DEPTH, MLA_NOPE), 0.02),
        "mla_kn_rope": 1.0 + nrm((DEPTH, MLA_ROPE), 0.02),
        "conv_w": nrm((DEPTH, CONV_W, CONV_DIM), CONV_W ** -0.5),
        "conv_b": nrm((DEPTH, CONV_DIM), 0.02),
        "w_out": nrm((DEPTH, MIX_DIM, D), MIX_DIM ** -0.5),
    }


def reference(x_prompt, x_sample, c_prompt, c_sample, cache_mla_latent, cache_mla_krope,
              state_rwkv, state_rwkv_shift, state_conv, w_ada, b_ada, norm_g, w_in,
              rw_mu, rw_w0, rw_w2, rw_a0, rw_a2, rw_kk, rw_ka, rw_rk, rw_ln_g, rw_ln_b,
              mla_qnorm_g, mla_kvnorm_g, mla_w_uq, mla_w_uk, mla_w_uv, mla_qn_nope,
              mla_qn_rope, mla_kn_nope, mla_kn_rope, conv_w, conv_b, w_out):
    yp, ys = x_prompt, x_sample
    Bp = x_prompt.shape[0]
    dt = x_prompt.dtype
    lat_p, kr_p, rw_p, sh_p, cv_p = [], [], [], [], []
    lat_s, kr_s, rw_s, sh_s, cv_s = [], [], [], [], []
    for l in range(DEPTH):
        lp = dict(w_ada=w_ada[l], b_ada=b_ada[l], norm_g=norm_g[l], w_in=w_in[l],
                  rw_mu=rw_mu[l], rw_w0=rw_w0[l], rw_w2=rw_w2[l], rw_a0=rw_a0[l],
                  rw_a2=rw_a2[l], rw_kk=rw_kk[l], rw_ka=rw_ka[l], rw_rk=rw_rk[l],
                  rw_ln_g=rw_ln_g[l], rw_ln_b=rw_ln_b[l], mla_qnorm_g=mla_qnorm_g[l],
                  mla_kvnorm_g=mla_kvnorm_g[l], mla_w_uq=mla_w_uq[l], mla_w_uk=mla_w_uk[l],
                  mla_w_uv=mla_w_uv[l], mla_qn_nope=mla_qn_nope[l], mla_qn_rope=mla_qn_rope[l],
                  mla_kn_nope=mla_kn_nope[l], mla_kn_rope=mla_kn_rope[l], conv_w=conv_w[l],
                  conv_b=conv_b[l], w_out=w_out[l])
        yp, lat, kr, rw, sh, cv = mixer_layer(
            yp, c_prompt,
            jnp.zeros((Bp, RWKV_HEADS, RWKV_HEAD, RWKV_HEAD), jnp.float32),
            jnp.zeros((Bp, RWKV_SHIFT_DIM), dt),
            jnp.zeros((Bp, CONV_W - 1, CONV_DIM), dt),
            jnp.zeros((Bp, 0, KV_LORA), dt),
            jnp.zeros((Bp, 0, MLA_ROPE), dt), **lp)
        lat_p.append(lat); kr_p.append(kr); rw_p.append(rw); sh_p.append(sh); cv_p.append(cv)
        ys, lat, kr, rw, sh, cv = mixer_layer(
            ys, c_sample, state_rwkv[l], state_rwkv_shift[l], state_conv[l],
            cache_mla_latent[l], cache_mla_krope[l], **lp)
        lat_s.append(lat); kr_s.append(kr); rw_s.append(rw); sh_s.append(sh); cv_s.append(cv)
    return (yp, ys,
            jnp.stack(lat_p), jnp.stack(kr_p), jnp.stack(rw_p), jnp.stack(sh_p), jnp.stack(cv_p),
            jnp.stack(lat_s), jnp.stack(kr_s), jnp.stack(rw_s), jnp.stack(sh_s), jnp.stack(cv_s))
```

```python
import functools
import math

import jax
import jax.numpy as jnp
from jax import lax
from jax.experimental import pallas as pl
from jax.experimental.pallas import tpu as pltpu

F32 = jnp.float32
BF16 = jnp.bfloat16

CHUNK = 64
RW_HEAD = 64
RW_LORA = 64
RW_GN_EPS = 64e-5
MLA_HEADS = 16
MLA_NOPE = 128
MLA_ROPE = 64
MLA_VHEAD = 128
ROPE_BASE = 10000.0
CONV_W = 3
NORM_EPS = 1e-6
MASK_VALUE = -1e30

LANES = 128
RW_L = 64
VMEM_LIMIT = 56 * 1024 * 1024
ROW_TILE = 512


def _cp(dims, vmem=VMEM_LIMIT):
    return pltpu.CompilerParams(dimension_semantics=dims, vmem_limit_bytes=vmem)


def _tile(n, pref, mult=8):
    t = min(pref, n)
    t -= t % mult
    while t >= mult:
        if n % t == 0:
            return t
        t -= mult
    return n


def _dot(a, b):
    return jnp.dot(a.astype(BF16), b.astype(BF16), preferred_element_type=F32)


def _dot_nt(a, b):
    return lax.dot_general(a.astype(BF16), b.astype(BF16), (((1,), (1,)), ((), ())),
                           preferred_element_type=F32)


def _dot_tn(a, b):
    return lax.dot_general(a.astype(BF16), b.astype(BF16), (((0,), (0,)), ((), ())),
                           preferred_element_type=F32)


def _split2(x):
    hi = x.astype(BF16)
    lo = (x - hi.astype(F32)).astype(BF16)
    return hi, lo


def _split3(x):
    hi = x.astype(BF16)
    r1 = x - hi.astype(F32)
    mid = r1.astype(BF16)
    lo = (r1 - mid.astype(F32)).astype(BF16)
    return hi, mid, lo


def _silu(x):
    return x / (1.0 + jnp.exp(-x))


def _swap_halves(x, seg):
    n = x.shape[-1]
    lane = lax.broadcasted_iota(jnp.int32, x.shape, x.ndim - 1)
    half = seg // 2
    fwd = pltpu.roll(x, n - half, axis=x.ndim - 1)
    bwd = pltpu.roll(x, half, axis=x.ndim - 1)
    return jnp.where((lane % seg) < half, fwd, bwd)


def _seg_sum(x, bd_ref):
    bd = bd_ref[...]
    outs = []
    for p in range(x.shape[-1] // LANES):
        hi, lo = _split2(x[:, p * LANES:(p + 1) * LANES])
        outs.append(jnp.dot(hi, bd, preferred_element_type=F32)
                    + jnp.dot(lo, bd, preferred_element_type=F32))
    return jnp.concatenate(outs, axis=-1) if len(outs) > 1 else outs[0]


def _ada_kernel(c_ref, w_ref, b_ref, o_ref):
    o_ref[0] = jnp.dot(c_ref[...], w_ref[0].astype(BF16), preferred_element_type=F32) + b_ref[0]


def _ada(c_bf16, w_ada, b_ada):
    nl, d, n3 = w_ada.shape
    rows = c_bf16.shape[0]
    tn = _tile(n3, 512, LANES)
    return pl.pallas_call(
        _ada_kernel,
        grid=(nl, n3 // tn),
        in_specs=[
            pl.BlockSpec((rows, d), lambda l, j: (0, 0)),
            pl.BlockSpec((1, d, tn), lambda l, j: (l, 0, j)),
            pl.BlockSpec((1, 1, tn), lambda l, j: (l, 0, j)),
        ],
        out_specs=pl.BlockSpec((1, rows, tn), lambda l, j: (l, 0, j)),
        out_shape=jax.ShapeDtypeStruct((nl, rows, n3), F32),
        compiler_params=_cp(("arbitrary", "arbitrary")),
        name="ada",
    )(c_bf16, w_ada, b_ada.reshape(nl, 1, n3))


def _proj_kernel(x_ref, g_ref, sc_ref, sh_ref, w_ref, o_ref, h_ref):
    bb, tt, d = x_ref.shape

    @pl.when(pl.program_id(2) == 0)
    def _():
        x = x_ref[...]
        ms = jnp.mean(x * x, axis=-1, keepdims=True)
        y = x * lax.rsqrt(ms + NORM_EPS) * g_ref[...]
        h = y * (1.0 + sc_ref[...]) + sh_ref[...]
        h_ref[...] = h.reshape(bb * tt, d).astype(BF16)

    acc = jnp.dot(h_ref[...], w_ref[...], preferred_element_type=F32)
    o_ref[...] = acc.reshape(bb, tt, acc.shape[-1])


def _proj(x, norm_g, scale, shift, w_in_p, bb, tt):
    b, t, d = x.shape
    n = w_in_p.shape[1]
    tn = _tile(n, 1024, LANES)
    return pl.pallas_call(
        _proj_kernel,
        grid=(b // bb, t // tt, n // tn),
        in_specs=[
            pl.BlockSpec((bb, tt, d), lambda i, j, k: (i, j, 0)),
            pl.BlockSpec((1, 1, d), lambda i, j, k: (0, 0, 0)),
            pl.BlockSpec((bb, 1, d), lambda i, j, k: (i, 0, 0)),
            pl.BlockSpec((bb, 1, d), lambda i, j, k: (i, 0, 0)),
            pl.BlockSpec((d, tn), lambda i, j, k: (0, k)),
        ],
        out_specs=pl.BlockSpec((bb, tt, tn), lambda i, j, k: (i, j, k)),
        out_shape=jax.ShapeDtypeStruct((b, t, n), F32),
        scratch_shapes=[pltpu.VMEM((bb * tt, d), BF16)],
        compiler_params=_cp(("arbitrary", "arbitrary", "arbitrary")),
        name="proj",
    )(x, norm_g.reshape(1, 1, d), scale, shift, w_in_p)


def _rwkv_kernel(r_ref, k_ref, v_ref, la_ref, g_ref,
                 shr_ref, shk_ref, shv_ref, shla_ref, s0_ref,
                 mur_ref, muk_ref, muv_ref, mula_ref, w2a2_ref,
                 w0_ref, a0_ref, kkg_ref, kag_ref, rk_ref, lng_ref, lnb_ref, bd_ref,
                 out_ref, sout_ref,
                 s_scr, pr_scr, pk_scr, pv_scr, pla_scr, *, t_valid):
    L = RW_L
    t_idx = pl.program_id(1)
    n_pairs = s_scr.shape[0]
    rd = r_ref.shape[-1]

    @pl.when(t_idx == 0)
    def _():
        s_scr[...] = s0_ref[0]
        pr_scr[...] = shr_ref[0]
        pk_scr[...] = shk_ref[0]
        pv_scr[...] = shv_ref[0]
        pla_scr[...] = shla_ref[0]

    def token_shift(x, prev_ref, mu_ref):
        row = lax.broadcasted_iota(jnp.int32, x.shape, 0)
        prev = jnp.where(row == 0, prev_ref[...], pltpu.roll(x, 1, axis=0))
        prev_ref[...] = x[L - 1:L, :]
        return x + (prev - x) * mu_ref[...]

    xr = token_shift(r_ref[0], pr_scr, mur_ref)
    xk = token_shift(k_ref[0], pk_scr, muk_ref)
    xv = token_shift(v_ref[0], pv_scr, muv_ref)
    xla = token_shift(la_ref[0], pla_scr, mula_ref)

    lane_la = lax.broadcasted_iota(jnp.int32, xla.shape, 1)
    la_in = jnp.where(lane_la < RW_LORA, jnp.tanh(xla), xla)
    wa = jnp.dot(la_in.astype(BF16), w2a2_ref[...], preferred_element_type=F32)
    wpre = wa[:, :rd] + w0_ref[...]
    apre = wa[:, rd:] + a0_ref[...]
    nz = -wpre
    softplus = jnp.maximum(nz, 0.0) + jnp.log(1.0 + jnp.exp(-jnp.abs(nz)))
    lw = -jnp.exp(-softplus - 0.5)
    a = 1.0 / (1.0 + jnp.exp(-apre))

    kk = xk * kkg_ref[...]
    kk = kk * lax.rsqrt(_seg_sum(kk * kk, bd_ref) + 1e-12)
    kmod = xk * (1.0 + (a - 1.0) * kag_ref[...])
    bvec = kk * a

    row_full = lax.broadcasted_iota(jnp.int32, (L, rd), 0) + t_idx * L
    live = row_full < t_valid
    lw = jnp.where(live, lw, 0.0)
    kk = jnp.where(live, kk, 0.0)
    bvec = jnp.where(live, bvec, 0.0)
    kmod_s = jnp.where(live, kmod, 0.0)
    v_s = jnp.where(live, xv, 0.0)

    ri = lax.broadcasted_iota(jnp.int32, (L, L), 0)
    ci = lax.broadcasted_iota(jnp.int32, (L, L), 1)
    tri = jnp.where(ci <= ri, 1.0, 0.0).astype(BF16)
    l_hi, l_mid, l_lo = _split3(lw)
    cum = (jnp.dot(tri, l_hi, preferred_element_type=F32)
           + jnp.dot(tri, l_mid, preferred_element_type=F32)
           + jnp.dot(tri, l_lo, preferred_element_type=F32))
    tot = cum[L - 1:L, :]
    e_pos = jnp.exp(cum)
    e_neg = jnp.exp(-cum)
    e_prev = jnp.exp(cum - lw)
    e_tail = jnp.exp(tot - cum)
    e_tot = jnp.exp(tot)

    at_f = -kk * e_prev
    rt_f = xr * e_pos
    bh_f = bvec * e_neg
    kh_f = kmod_s * e_neg
    bb_f = bvec * e_tail
    kb_f = kmod_s * e_tail

    lane = lax.broadcasted_iota(jnp.int32, (L, LANES), 1)
    rowl = lax.broadcasted_iota(jnp.int32, (L, LANES), 0)
    m0 = lane < RW_HEAD
    strict2 = (lane % RW_HEAD) < rowl
    incl2 = (lane % RW_HEAD) <= rowl
    r2 = lax.broadcasted_iota(jnp.int32, (2 * L, LANES), 0)
    c2 = lax.broadcasted_iota(jnp.int32, (2 * L, LANES), 1)
    eye2 = jnp.where(r2 == c2, 1.0, 0.0)
    bdm = (r2 < RW_HEAD) == (c2 < RW_HEAD)
    zeros_l = jnp.zeros((L, LANES), F32)

    o_tiles = []
    for p in range(n_pairs):
        sl = slice(p * LANES, (p + 1) * LANES)
        at, rt, bh, kh = at_f[:, sl], rt_f[:, sl], bh_f[:, sl], kh_f[:, sl]
        bb_, kb, vp = bb_f[:, sl], kb_f[:, sl], v_s[:, sl]
        sblk = s_scr[p]

        lhs4 = jnp.concatenate([jnp.where(m0, at, 0.0), jnp.where(m0, 0.0, at),
                                jnp.where(m0, rt, 0.0), jnp.where(m0, 0.0, rt)], axis=0)
        bk4 = jnp.concatenate([bh, kh, kh, bh], axis=0)
        l_h, l_l = _split2(lhs4)
        b_h, b_l = _split2(bk4)
        aa = _dot_nt(l_h, b_h) + _dot_nt(l_h, b_l) + _dot_nt(l_l, b_h)

        x0 = jnp.where(strict2, aa[0:L, 0:LANES], 0.0)
        x1 = jnp.where(strict2, aa[L:2 * L, LANES:], 0.0)
        y0 = jnp.where(incl2, aa[2 * L:3 * L, 0:LANES], 0.0)
        y1 = jnp.where(incl2, aa[3 * L:, LANES:], 0.0)

        n_bd = jnp.concatenate([jnp.where(m0, x0, 0.0), jnp.where(m0, 0.0, x1)], axis=0)
        ak_bd = jnp.concatenate([jnp.where(m0, 0.0, x0), jnp.where(m0, x1, 0.0)], axis=0)

        pinv = eye2 + n_bd
        npow = n_bd
        for _ in range(int(math.log2(L)) - 1):
            npow = _dot(npow, npow)
            pinv = pinv + _dot(npow, pinv)

        sr = _dot_nt(jnp.concatenate([at, rt], axis=0), sblk)
        a_s, r_s = sr[0:L], sr[L:]
        rhs = _dot(ak_bd, jnp.concatenate([vp, vp], axis=0)) + jnp.concatenate([a_s, a_s], axis=0)
        z_st = _dot(pinv, rhs)
        z = jnp.where(m0, z_st[0:L], z_st[L:])

        lhs_o = jnp.concatenate([jnp.concatenate([y0, zeros_l], axis=1),
                                 jnp.concatenate([zeros_l, y1], axis=1)], axis=0)
        rhs_o = jnp.concatenate([z, vp, vp, z], axis=0)
        o_st = _dot(lhs_o, rhs_o)
        o_tiles.append(jnp.where(m0, o_st[0:L], o_st[L:]) + r_s)

        upd = _dot_tn(jnp.concatenate([z, vp], axis=0), jnp.concatenate([bb_, kb], axis=0))
        s_scr[p] = sblk * e_tot[:, sl] + jnp.where(bdm, upd, 0.0)

    o = jnp.concatenate(o_tiles, axis=-1)
    inv_n = 1.0 / RW_HEAD
    mu = _seg_sum(o, bd_ref) * inv_n
    oc = o - mu
    var = _seg_sum(oc * oc, bd_ref) * inv_n
    on = oc * lax.rsqrt(var + RW_GN_EPS) * lng_ref[...] + lnb_ref[...]
    bonus = _seg_sum(xr * kmod * rk_ref[...], bd_ref) * xv
    out_ref[0] = ((on + bonus) * _silu(g_ref[0])).astype(out_ref.dtype)

    @pl.when(t_idx == pl.num_programs(1) - 1)
    def _():
        sout_ref[0] = s_scr[...]


def _rwkv(proj, offs, shift0, s0_blk, lp, t_valid):
    b, t, _ = proj.shape
    rd = lp["mu_r"].shape[-1]
    n_pairs = rd // LANES
    L = RW_L
    cr, ck, cv, cg, cla = (offs["r"] // rd, offs["k"] // rd, offs["v"] // rd,
                           offs["rw_gate"] // rd, offs["la"] // LANES)

    def col(c, w):
        return pl.BlockSpec((1, L, w), lambda i, j, c=c: (i, j, c))

    def per_b(w):
        return pl.BlockSpec((1, 1, w), lambda i, j: (i, 0, 0))

    def const(shape):
        return pl.BlockSpec(shape, lambda i, j: (0,) * len(shape))

    vec = const((1, rd))
    out, s_out = pl.pallas_call(
        functools.partial(_rwkv_kernel, t_valid=t_valid),
        grid=(b, t // L),
        in_specs=[col(cr, rd), col(ck, rd), col(cv, rd), col(cla, LANES), col(cg, rd),
                  per_b(rd), per_b(rd), per_b(rd), per_b(LANES),
                  pl.BlockSpec((1, n_pairs, LANES, LANES), lambda i, j: (i, 0, 0, 0)),
                  vec, vec, vec, const((1, LANES)), const((LANES, 2 * rd)),
                  vec, vec, vec, vec, vec, vec, vec, const((LANES, LANES))],
        out_specs=[pl.BlockSpec((1, L, rd), lambda i, j: (i, j, 0)),
                   pl.BlockSpec((1, n_pairs, LANES, LANES), lambda i, j: (i, 0, 0, 0))],
        out_shape=[jax.ShapeDtypeStruct((b, t, rd), BF16),
                   jax.ShapeDtypeStruct((b, n_pairs, LANES, LANES), F32)],
        scratch_shapes=[pltpu.VMEM((n_pairs, LANES, LANES), F32),
                        pltpu.VMEM((1, rd), F32), pltpu.VMEM((1, rd), F32),
                        pltpu.VMEM((1, rd), F32), pltpu.VMEM((1, LANES), F32)],
        compiler_params=_cp(("arbitrary", "arbitrary")),
        name="rwkv",
    )(proj, proj, proj, proj, proj,
      shift0["r"], shift0["k"], shift0["v"], shift0["la"], s0_blk,
      lp["mu_r"], lp["mu_k"], lp["mu_v"], lp["mu_la"], lp["w2a2"],
      lp["w0"], lp["a0"], lp["kk_g"], lp["ka_g"], lp["rk"], lp["ln_g"], lp["ln_b"], lp["bd"])
    return out, s_out


def _latkr_kernel(ckv_ref, kr_ref, gkv_ref, gkr_ref, cos_ref, sin_ref, lat_ref, kro_ref):
    x = ckv_ref[0]
    lat_ref[0] = x * lax.rsqrt(jnp.mean(x * x, axis=-1, keepdims=True) + NORM_EPS) * gkv_ref[...]
    y = kr_ref[0]
    ms = jnp.sum(y * y, axis=-1, keepdims=True) * (1.0 / MLA_ROPE)
    yn = y * lax.rsqrt(ms + NORM_EPS) * gkr_ref[...]
    rot = yn * cos_ref[...] + _swap_halves(yn, MLA_ROPE) * sin_ref[...]
    kro_ref[0] = rot[:, :MLA_ROPE]


def _latkr(proj, offs, lp, cos_t, sin_t, tt):
    b, t, _ = proj.shape
    kvl = lp["kvnorm_g"].shape[-1]
    return pl.pallas_call(
        _latkr_kernel,
        grid=(b, t // tt),
        in_specs=[pl.BlockSpec((1, tt, kvl), lambda i, j: (i, j, offs["ckv"] // kvl)),
                  pl.BlockSpec((1, tt, LANES), lambda i, j: (i, j, offs["kr"] // LANES)),
                  pl.BlockSpec((1, kvl), lambda i, j: (0, 0)),
                  pl.BlockSpec((1, LANES), lambda i, j: (0, 0)),
                  pl.BlockSpec((tt, LANES), lambda i, j: (j, 0)),
                  pl.BlockSpec((tt, LANES), lambda i, j: (j, 0))],
        out_specs=[pl.BlockSpec((1, tt, kvl), lambda i, j: (i, j, 0)),
                   pl.BlockSpec((1, tt, MLA_ROPE), lambda i, j: (i, j, 0))],
        out_shape=[jax.ShapeDtypeStruct((b, t, kvl), F32),
                   jax.ShapeDtypeStruct((b, t, MLA_ROPE), F32)],
        compiler_params=_cp(("arbitrary", "arbitrary")),
        name="latkr",
    )(proj, proj, lp["kvnorm_g"], lp["kn_rope_g"], cos_t, sin_t)


def _qproj_kernel(cq_ref, gq_ref, w_ref, gn_ref, gr_ref, cos_ref, sin_ref, bd_ref,
                  qn_ref, qr_ref):
    x = cq_ref[0]
    xn = x * lax.rsqrt(jnp.mean(x * x, axis=-1, keepdims=True) + NORM_EPS) * gq_ref[...]
    q = jnp.dot(xn.astype(BF16), w_ref[...], preferred_element_type=F32)
    n_nope = qn_ref.shape[-1]
    for h in range(n_nope // MLA_NOPE):
        sl = slice(h * MLA_NOPE, (h + 1) * MLA_NOPE)
        qh = q[:, sl]
        qh = qh * lax.rsqrt(jnp.mean(qh * qh, axis=-1, keepdims=True) + NORM_EPS) * gn_ref[:, sl]
        qn_ref[0, :, sl] = qh.astype(qn_ref.dtype)
    qr = q[:, n_nope:]
    ms = _seg_sum(qr * qr, bd_ref) * (1.0 / MLA_ROPE)
    qr = qr * lax.rsqrt(ms + NORM_EPS) * gr_ref[...]
    reps = qr.shape[-1] // LANES
    cos = jnp.concatenate([cos_ref[...]] * reps, axis=-1)
    sin = jnp.concatenate([sin_ref[...]] * reps, axis=-1)
    qr_ref[0] = (qr * cos + _swap_halves(qr, MLA_ROPE) * sin).astype(qr_ref.dtype)


def _qproj(proj, offs, lp, cos_t, sin_t, tt):
    b, t, _ = proj.shape
    ql = lp["qnorm_g"].shape[-1]
    n_all = lp["w_uq"].shape[1]
    n_nope = MLA_HEADS * MLA_NOPE
    n_rope = MLA_HEADS * MLA_ROPE
    return pl.pallas_call(
        _qproj_kernel,
        grid=(b, t // tt),
        in_specs=[pl.BlockSpec((1, tt, ql), lambda i, j: (i, j, offs["cq"] // ql)),
                  pl.BlockSpec((1, ql), lambda i, j: (0, 0)),
                  pl.BlockSpec((ql, n_all), lambda i, j: (0, 0)),
                  pl.BlockSpec((1, n_nope), lambda i, j: (0, 0)),
                  pl.BlockSpec((1, n_rope), lambda i, j: (0, 0)),
                  pl.BlockSpec((tt, LANES), lambda i, j: (j, 0)),
                  pl.BlockSpec((tt, LANES), lambda i, j: (j, 0)),
                  pl.BlockSpec((LANES, LANES), lambda i, j: (0, 0))],
        out_specs=[pl.BlockSpec((1, tt, n_nope), lambda i, j: (i, j, 0)),
                   pl.BlockSpec((1, tt, n_rope), lambda i, j: (i, j, 0))],
        out_shape=[jax.ShapeDtypeStruct((b, t, n_nope), BF16),
                   jax.ShapeDtypeStruct((b, t, n_rope), BF16)],
        compiler_params=_cp(("arbitrary", "arbitrary")),
        name="qproj",
    )(proj, lp["qnorm_g"], lp["w_uq"], lp["qn_nope_g"], lp["qn_rope_g"], cos_t, sin_t, lp["bd"])


def _kv_kernel(lat_ref, wk_ref, wv_ref, gk_ref, k_ref, v_ref):
    lat = lat_ref[0].astype(BF16)
    kf = jnp.dot(lat, wk_ref[...], preferred_element_type=F32)
    for h in range(kf.shape[-1] // MLA_NOPE):
        sl = slice(h * MLA_NOPE, (h + 1) * MLA_NOPE)
        kh = kf[:, sl]
        kh = kh * lax.rsqrt(jnp.mean(kh * kh, axis=-1, keepdims=True) + NORM_EPS) * gk_ref[:, sl]
        k_ref[0, :, sl] = kh.astype(k_ref.dtype)
    v_ref[0] = jnp.dot(lat, wv_ref[...], preferred_element_type=F32).astype(v_ref.dtype)


def _kv(lat_all, lp, tt):
    b, tk, kvl = lat_all.shape
    nk = lp["w_uk"].shape[1]
    nv = lp["w_uv"].shape[1]
    return pl.pallas_call(
        _kv_kernel,
        grid=(b, tk // tt),
        in_specs=[pl.BlockSpec((1, tt, kvl), lambda i, j: (i, j, 0)),
                  pl.BlockSpec((kvl, nk), lambda i, j: (0, 0)),
                  pl.BlockSpec((kvl, nv), lambda i, j: (0, 0)),
                  pl.BlockSpec((1, nk), lambda i, j: (0, 0))],
        out_specs=[pl.BlockSpec((1, tt, nk), lambda i, j: (i, j, 0)),
                   pl.BlockSpec((1, tt, nv), lambda i, j: (i, j, 0))],
        out_shape=[jax.ShapeDtypeStruct((b, tk, nk), BF16),
                   jax.ShapeDtypeStruct((b, tk, nv), BF16)],
        compiler_params=_cp(("arbitrary", "arbitrary")),
        name="kv",
    )(lat_all, lp["w_uk"], lp["w_uv"], lp["kn_nope_g"])


def _attn_kernel(qn_ref, qr_ref, kn_ref, kr_ref, v_ref, g_ref, o_ref,
                 qrm_scr, m_scr, l_scr, acc_scr, *, past, kv_len, tq, tk):
    h = pl.program_id(1)
    qi = pl.program_id(2)
    ki = pl.program_id(3)

    @pl.when(ki == 0)
    def _():
        lane = lax.broadcasted_iota(jnp.int32, (tq, LANES), 1)
        mine = jnp.where(lane < MLA_ROPE, 0, 1) == h % 2
        qrm_scr[...] = jnp.where(mine, qr_ref[0], jnp.zeros_like(qr_ref[0]))
        m_scr[...] = jnp.full(m_scr.shape, -jnp.inf, F32)
        l_scr[...] = jnp.zeros(l_scr.shape, F32)
        acc_scr[...] = jnp.zeros(acc_scr.shape, F32)

    q_chunk_max = (past + (qi + 1) * tq - 1) // CHUNK

    @pl.when(ki * tk <= q_chunk_max * CHUNK + (CHUNK - 1))
    def _():
        s = (lax.dot_general(qn_ref[0], kn_ref[0], (((1,), (1,)), ((), ())),
                             preferred_element_type=F32)
             + lax.dot_general(qrm_scr[...], kr_ref[0], (((1,), (1,)), ((), ())),
                               preferred_element_type=F32))
        q_pos = past + qi * tq + lax.broadcasted_iota(jnp.int32, (tq, tk), 0)
        k_pos = ki * tk + lax.broadcasted_iota(jnp.int32, (tq, tk), 1)
        vis = jnp.logical_and(k_pos // CHUNK <= q_pos // CHUNK, k_pos < kv_len)
        s = jnp.where(vis, s, MASK_VALUE)
        m_prev = m_scr[:, :1]
        m_new = jnp.maximum(m_prev, jnp.max(s, axis=-1, keepdims=True))
        alpha = jnp.exp(m_prev - m_new)
        p = jnp.exp(s - m_new)
        l_scr[...] = jnp.broadcast_to(alpha * l_scr[:, :1] + jnp.sum(p, axis=-1, keepdims=True),
                                      l_scr.shape)
        acc_scr[...] = alpha * acc_scr[...] + jnp.dot(p.astype(BF16), v_ref[0],
                                                      preferred_element_type=F32)
        m_scr[...] = jnp.broadcast_to(m_new, m_scr.shape)

    @pl.when(ki == pl.num_programs(3) - 1)
    def _():
        o = acc_scr[...] / l_scr[:, :1]
        o_ref[0] = (o * _silu(g_ref[0])).astype(o_ref.dtype)


def _attn(qn, qr, kn, krd, v, proj, offs, past, kv_len, tq, tk):
    b, t, _ = qn.shape
    tkp = kn.shape[1]
    nq, nk = t // tq, tkp // tk
    gate_c0 = offs["mla_gate"] // LANES

    def last_k(qi):
        q_chunk_max = (past + (qi + 1) * tq - 1) // CHUNK
        return jnp.minimum((q_chunk_max * CHUNK + CHUNK - 1) // tk, nk - 1)

    def kv_map(bi, h, qi, ki):
        return (bi, jnp.minimum(ki, last_k(qi)), h)

    def kr_map(bi, h, qi, ki):
        return (bi, jnp.minimum(ki, last_k(qi)), 0)

    return pl.pallas_call(
        functools.partial(_attn_kernel, past=past, kv_len=kv_len, tq=tq, tk=tk),
        grid=(b, MLA_HEADS, nq, nk),
        in_specs=[pl.BlockSpec((1, tq, MLA_NOPE), lambda bi, h, qi, ki: (bi, qi, h)),
                  pl.BlockSpec((1, tq, LANES), lambda bi, h, qi, ki: (bi, qi, h // 2)),
                  pl.BlockSpec((1, tk, MLA_NOPE), kv_map),
                  pl.BlockSpec((1, tk, LANES), kr_map),
                  pl.BlockSpec((1, tk, MLA_VHEAD), kv_map),
                  pl.BlockSpec((1, tq, MLA_VHEAD), lambda bi, h, qi, ki: (bi, qi, gate_c0 + h))],
        out_specs=pl.BlockSpec((1, tq, MLA_VHEAD), lambda bi, h, qi, ki: (bi, qi, h)),
        out_shape=jax.ShapeDtypeStruct((b, t, MLA_HEADS * MLA_VHEAD), BF16),
        scratch_shapes=[pltpu.VMEM((tq, LANES), BF16),
                        pltpu.VMEM((tq, LANES), F32), pltpu.VMEM((tq, LANES), F32),
                        pltpu.VMEM((tq, MLA_VHEAD), F32)],
        compiler_params=_cp(("arbitrary", "arbitrary", "arbitrary", "arbitrary")),
        name="attn",
    )(qn, qr, kn, krd, v, proj)


def _conv_kernel(cb_ref, cc_ref, cx_ref, cg_ref, st_ref, w_ref, b_ref, o_ref, nc_ref, carry,
                 *, t_valid):
    tt = cc_ref.shape[1]
    j = pl.program_id(1)

    @pl.when(j == 0)
    def _():
        carry[...] = st_ref[0]

    u = cc_ref[0] * cx_ref[0]
    row = lax.broadcasted_iota(jnp.int32, u.shape, 0)
    c0, c1 = carry[0:1, :], carry[1:2, :]
    u1 = jnp.where(row == 0, c1, pltpu.roll(u, 1, axis=0))
    u2 = jnp.where(row == 0, c0, jnp.where(row == 1, c1, pltpu.roll(u, 2, axis=0)))
    y = b_ref[...] + u2 * w_ref[0:1, :] + u1 * w_ref[1:2, :] + u * w_ref[2:3, :]
    o_ref[0] = (cb_ref[0] * y * _silu(cg_ref[0])).astype(o_ref.dtype)
    carry[...] = u[tt - 2:tt, :]

    @pl.when(j == (t_valid - 1) // tt)
    def _():
        lr = (t_valid - 1) % tt
        nc_ref[0] = u[lr - 1:lr + 1, :]


def _conv(proj, offs, conv_state, lp, tt, t_valid):
    b, t, _ = proj.shape
    cd = lp["conv_b"].shape[-1]

    def col(name):
        return pl.BlockSpec((1, tt, cd), lambda i, j, c=offs[name] // cd: (i, j, c))

    return pl.pallas_call(
        functools.partial(_conv_kernel, t_valid=t_valid),
        grid=(b, t // tt),
        in_specs=[col("cv_b"), col("cv_c"), col("cv_x"), col("cv_gate"),
                  pl.BlockSpec((1, CONV_W - 1, cd), lambda i, j: (i, 0, 0)),
                  pl.BlockSpec((CONV_W, cd), lambda i, j: (0, 0)),
                  pl.BlockSpec((1, cd), lambda i, j: (0, 0))],
        out_specs=[pl.BlockSpec((1, tt, cd), lambda i, j: (i, j, 0)),
                   pl.BlockSpec((1, CONV_W - 1, cd), lambda i, j: (i, 0, 0))],
        out_shape=[jax.ShapeDtypeStruct((b, t, cd), BF16),
                   jax.ShapeDtypeStruct((b, CONV_W - 1, cd), F32)],
        scratch_shapes=[pltpu.VMEM((CONV_W - 1, cd), F32)],
        compiler_params=_cp(("arbitrary", "arbitrary")),
        name="conv",
    )(proj, proj, proj, proj, conv_state, lp["conv_w"], lp["conv_b"])


def _outproj_kernel(rw_ref, mla_ref, cv_ref, w1_ref, w2_ref, w3_ref, x_ref, gt_ref, o_ref):
    bb, tt, tn = x_ref.shape

    def flat(r):
        return r[...].reshape(bb * tt, r.shape[-1])

    acc = (jnp.dot(flat(rw_ref), w1_ref[...], preferred_element_type=F32)
           + jnp.dot(flat(mla_ref), w2_ref[...], preferred_element_type=F32)
           + jnp.dot(flat(cv_ref), w3_ref[...], preferred_element_type=F32))
    o_ref[...] = x_ref[...] + gt_ref[...] * acc.reshape(bb, tt, tn)


def _outproj(rw, mla, cv, lp, x, gate, bb, tt):
    b, t, d = x.shape
    tn = _tile(d, 1024, LANES)
    d1, d2, d3 = rw.shape[-1], mla.shape[-1], cv.shape[-1]
    return pl.pallas_call(
        _outproj_kernel,
        grid=(b // bb, t // tt, d // tn),
        in_specs=[pl.BlockSpec((bb, tt, d1), lambda i, j, k: (i, j, 0)),
                  pl.BlockSpec((bb, tt, d2), lambda i, j, k: (i, j, 0)),
                  pl.BlockSpec((bb, tt, d3), lambda i, j, k: (i, j, 0)),
                  pl.BlockSpec((d1, tn), lambda i, j, k: (0, k)),
                  pl.BlockSpec((d2, tn), lambda i, j, k: (0, k)),
                  pl.BlockSpec((d3, tn), lambda i, j, k: (0, k)),
                  pl.BlockSpec((bb, tt, tn), lambda i, j, k: (i, j, k)),
                  pl.BlockSpec((bb, 1, tn), lambda i, j, k: (i, 0, k))],
        out_specs=pl.BlockSpec((bb, tt, tn), lambda i, j, k: (i, j, k)),
        out_shape=jax.ShapeDtypeStruct((b, t, d), F32),
        compiler_params=_cp(("arbitrary", "arbitrary", "arbitrary")),
        name="outproj",
    )(rw, mla, cv, lp["w_out_rw"], lp["w_out_mla"], lp["w_out_cv"], x, gate)


def _layout(d):
    rd = d // 4
    q_lora = d // 4
    kv_lora = 512
    mla_dim = MLA_HEADS * MLA_VHEAD
    cd = d - rd - mla_dim
    shift_dim = 3 * rd + 2 * RW_LORA
    src, pos = {}, 0
    for name, w in (("rw_pre", shift_dim), ("rw_gate", rd), ("cq", q_lora), ("ckv", kv_lora),
                    ("kr", MLA_ROPE), ("mla_gate", mla_dim), ("cv_b", cd), ("cv_c", cd),
                    ("cv_x", cd), ("cv_gate", cd)):
        src[name] = (pos, w)
        pos += w
    p0 = src["rw_pre"][0]
    src["r"], src["k"], src["v"] = (p0, rd), (p0 + rd, rd), (p0 + 2 * rd, rd)
    src["la"] = (p0 + 3 * rd, 2 * RW_LORA)
    order = ("r", "k", "v", "rw_gate", "cq", "cv_b", "cv_c", "cv_x", "cv_gate", "mla_gate",
             "ckv", "la", "kr")
    offs, pos = {}, 0
    for name in order:
        offs[name] = pos
        pos += src[name][1]
    total = -(-pos // 1024) * 1024
    return src, order, offs, pos, total, dict(rd=rd, q_lora=q_lora, kv_lora=kv_lora, cd=cd,
                                             shift_dim=shift_dim, in_cols=src["cv_gate"][0] + cd)


def _prep_layer(l, src, order, used, total, dims, w):
    rd = dims["rd"]
    w_in = w["w_in"][l]
    cols = [w_in[:, src[n][0]:src[n][0] + src[n][1]] for n in order]
    cols.append(jnp.zeros((w_in.shape[0], total - used), w_in.dtype))
    w_in_p = jnp.concatenate(cols, axis=1).astype(BF16)

    w_uq = w["mla_w_uq"][l].reshape(-1, MLA_HEADS, MLA_NOPE + MLA_ROPE)
    w_uq_p = jnp.concatenate([w_uq[:, :, :MLA_NOPE].reshape(w_uq.shape[0], -1),
                              w_uq[:, :, MLA_NOPE:].reshape(w_uq.shape[0], -1)], axis=1).astype(BF16)
    zl = jnp.zeros((RW_LORA, rd), F32)
    w2a2 = jnp.concatenate([jnp.concatenate([w["rw_w2"][l], zl], axis=1),
                            jnp.concatenate([zl, w["rw_a2"][l]], axis=1)], axis=0).astype(BF16)
    mu = w["rw_mu"][l]
    scale = float(MLA_NOPE + MLA_ROPE) ** -0.5
    idx = jnp.arange(LANES)
    bd = ((idx[:, None] // RW_HEAD) == (idx[None, :] // RW_HEAD)).astype(BF16)
    w_out = w["w_out"][l].astype(BF16)
    mla_dim = MLA_HEADS * MLA_VHEAD
    row = lambda v: v.reshape(1, -1)
    return dict(
        w_in=w_in_p, norm_g=w["norm_g"][l],
        mu_r=row(mu[:rd]), mu_k=row(mu[rd:2 * rd]), mu_v=row(mu[2 * rd:3 * rd]), mu_la=row(mu[3 * rd:]),
        w2a2=w2a2, w0=row(w["rw_w0"][l]), a0=row(w["rw_a0"][l]), kk_g=row(w["rw_kk"][l]),
        ka_g=row(w["rw_ka"][l]), rk=row(w["rw_rk"][l]), ln_g=row(w["rw_ln_g"][l]),
        ln_b=row(w["rw_ln_b"][l]), bd=bd,
        qnorm_g=row(w["mla_qnorm_g"][l]), kvnorm_g=row(w["mla_kvnorm_g"][l]),
        w_uq=w_uq_p, w_uk=w["mla_w_uk"][l].astype(BF16), w_uv=w["mla_w_uv"][l].astype(BF16),
        qn_nope_g=row(jnp.tile(w["mla_qn_nope"][l], MLA_HEADS)) * scale,
        qn_rope_g=row(jnp.tile(w["mla_qn_rope"][l], MLA_HEADS)) * scale,
        kn_nope_g=row(jnp.tile(w["mla_kn_nope"][l], MLA_HEADS)),
        kn_rope_g=row(jnp.concatenate([w["mla_kn_rope"][l], jnp.zeros((LANES - MLA_ROPE,), F32)])),
        conv_w=w["conv_w"][l], conv_b=row(w["conv_b"][l]),
        w_out_rw=w_out[:rd], w_out_mla=w_out[rd:rd + mla_dim], w_out_cv=w_out[rd + mla_dim:],
    )


def _rope_tables(past, t):
    half = MLA_ROPE // 2
    freqs = ROPE_BASE ** (-jnp.arange(half, dtype=F32) / half)
    ang = jnp.arange(past, past + t, dtype=jnp.int32).astype(F32)[:, None] * freqs[None, :]
    cos, sin = jnp.cos(ang), jnp.sin(ang)
    reps = LANES // MLA_ROPE
    return (jnp.concatenate([cos, cos] * reps, axis=1), jnp.concatenate([-sin, sin] * reps, axis=1))


def _pad_rows(x, n):
    if x.shape[1] == n:
        return x
    pad = [(0, 0)] * x.ndim
    pad[1] = (0, n - x.shape[1])
    return jnp.pad(x, pad)


def _mixer(x, t_real, past, mod, rw_state, rw_shift, conv_state, lat_past, kr_past,
           lp, offs, dims):
    b, t, d = x.shape
    rd = dims["rd"]
    shift, scale, gate = mod
    tt = _tile(t, ROW_TILE)
    bb = _tile(b, max(1, ROW_TILE // tt), 1) if tt == t else 1

    proj = _proj(x, lp["norm_g"], scale, shift, lp["w_in"], bb, tt)

    n_pairs = rd // LANES
    if rw_state is None:
        s0_blk = jnp.zeros((b, n_pairs, LANES, LANES), F32)
        shift0 = dict(r=jnp.zeros((b, 1, rd), F32), k=jnp.zeros((b, 1, rd), F32),
                      v=jnp.zeros((b, 1, rd), F32), la=jnp.zeros((b, 1, LANES), F32))
    else:
        s4 = rw_state.astype(F32).reshape(b, n_pairs, 2, RW_HEAD, RW_HEAD)
        z = jnp.zeros_like(s4[:, :, 0])
        s0_blk = jnp.concatenate([jnp.concatenate([s4[:, :, 0], z], axis=-1),
                                  jnp.concatenate([z, s4[:, :, 1]], axis=-1)], axis=-2)
        sh = rw_shift[:, None, :]
        shift0 = dict(r=sh[..., :rd], k=sh[..., rd:2 * rd], v=sh[..., 2 * rd:3 * rd],
                      la=sh[..., 3 * rd:])
    rw_out, s_blk = _rwkv(proj, offs, shift0, s0_blk, lp, t_real)
    new_state = jnp.stack([s_blk[:, :, :RW_HEAD, :RW_HEAD], s_blk[:, :, RW_HEAD:, RW_HEAD:]],
                          axis=2).reshape(b, 2 * n_pairs, RW_HEAD, RW_HEAD)
    new_shift = jnp.concatenate([proj[:, t_real - 1, offs["r"]:offs["r"] + 3 * rd],
                                 proj[:, t_real - 1, offs["la"]:offs["la"] + 2 * RW_LORA]], axis=-1)

    cos_t, sin_t = _rope_tables(past, t)
    tt_s = _tile(t, 512)
    lat_new, kr_new = _latkr(proj, offs, lp, cos_t, sin_t, tt_s)
    qn, qr = _qproj(proj, offs, lp, cos_t, sin_t, tt_s)
    kv_len = past + t_real
    if past == 0 and t_real == t:
        lat_all, kr_all, tk_pad = lat_new, kr_new, t
    else:
        tk_pad = -(-kv_len // LANES) * LANES
        lat_all = _pad_rows(jnp.concatenate([lat_past.astype(BF16),
                                             lat_new[:, :t_real].astype(BF16)], axis=1), tk_pad)
        kr_all = _pad_rows(jnp.concatenate([kr_past, kr_new[:, :t_real]], axis=1), tk_pad)
    krd = jnp.concatenate([kr_all, kr_all], axis=-1).astype(BF16)
    kn, vv = _kv(lat_all, lp, _tile(tk_pad, 512, 16))
    tq = _tile(t, 512)
    tk = tk_pad if tk_pad <= 4096 and t < 256 else _tile(tk_pad, 512, LANES)
    mla_out = _attn(qn, qr, kn, krd, vv, proj, offs, past, kv_len, tq, tk)

    if conv_state is None:
        conv_state = jnp.zeros((b, CONV_W - 1, dims["cd"]), F32)
    cv_out, new_conv = _conv(proj, offs, conv_state, lp, tt_s, t_real)

    y = _outproj(rw_out, mla_out, cv_out, lp, x, gate, bb, tt)
    return (y, lat_new[:, :t_real], kr_new[:, :t_real], new_state, new_shift, new_conv)


def kernel(x_prompt, x_sample, c_prompt, c_sample, cache_mla_latent, cache_mla_krope, state_rwkv, state_rwkv_shift, state_conv, w_ada, b_ada, norm_g, w_in, rw_mu, rw_w0, rw_w2, rw_a0, rw_a2, rw_kk, rw_ka, rw_rk, rw_ln_g, rw_ln_b, mla_qnorm_g, mla_kvnorm_g, mla_w_uq, mla_w_uk, mla_w_uv, mla_qn_nope, mla_qn_rope, mla_kn_nope, mla_kn_rope, conv_w, conv_b, w_out):
    w = dict(norm_g=norm_g, w_in=w_in, rw_mu=rw_mu, rw_w0=rw_w0, rw_w2=rw_w2, rw_a0=rw_a0,
             rw_a2=rw_a2, rw_kk=rw_kk, rw_ka=rw_ka, rw_rk=rw_rk, rw_ln_g=rw_ln_g, rw_ln_b=rw_ln_b,
             mla_qnorm_g=mla_qnorm_g, mla_kvnorm_g=mla_kvnorm_g, mla_w_uq=mla_w_uq,
             mla_w_uk=mla_w_uk, mla_w_uv=mla_w_uv, mla_qn_nope=mla_qn_nope, mla_qn_rope=mla_qn_rope,
             mla_kn_nope=mla_kn_nope, mla_kn_rope=mla_kn_rope, conv_w=conv_w, conv_b=conv_b,
             w_out=w_out)
    depth = w_in.shape[0]
    bp, tp, d = x_prompt.shape
    bs, ts, _ = x_sample.shape
    past = cache_mla_latent.shape[2]
    src, order, offs, used, total, dims = _layout(d)
    assert dims["in_cols"] == w_in.shape[2] and tp % RW_L == 0

    rows = -(-(bp + bs) // 16) * 16
    c_all = jnp.concatenate([c_prompt, c_sample, jnp.zeros((rows - bp - bs, d), F32)], axis=0)
    mod = _ada(c_all.astype(BF16), w_ada, b_ada)

    def mods(l, lo, n):
        m = mod[l, lo:lo + n][:, None, :]
        return m[..., :d], m[..., d:2 * d], m[..., 2 * d:]

    ts_pad = -(-ts // RW_L) * RW_L
    yp, ys = x_prompt, _pad_rows(x_sample, ts_pad)
    outs_p, outs_s = [], []
    for l in range(depth):
        lp = _prep_layer(l, src, order, used, total, dims, w)
        rp = _mixer(yp, tp, 0, mods(l, 0, bp), None, None, None, None, None, lp, offs, dims)
        yp = rp[0]
        outs_p.append(rp[1:])
        rs = _mixer(ys, ts, past, mods(l, bp, bs), state_rwkv[l], state_rwkv_shift[l],
                    state_conv[l], cache_mla_latent[l], cache_mla_krope[l], lp, offs, dims)
        ys = rs[0]
        outs_s.append(rs[1:])

    def stack(outs, i):
        return jnp.stack([o[i] for o in outs])

    return (yp, ys[:, :ts],
            stack(outs_p, 0), stack(outs_p, 1), stack(outs_p, 2), stack(outs_p, 3), stack(outs_p, 4),
            stack(outs_s, 0), stack(outs_s, 1), stack(outs_s, 2), stack(outs_s, 3), stack(outs_s, 4))
```

```python
import functools
import math

import jax
import jax.numpy as jnp
from jax import lax
from jax.experimental import pallas as pl
from jax.experimental.pallas import tpu as pltpu

F32 = jnp.float32
BF16 = jnp.bfloat16

CHUNK = 64
RW_HEAD = 64
RW_LORA = 64
RW_GN_EPS = 64e-5
MLA_HEADS = 16
MLA_NOPE = 128
MLA_ROPE = 64
MLA_VHEAD = 128
ROPE_BASE = 10000.0
CONV_W = 3
NORM_EPS = 1e-6
MASK_VALUE = -1e30

LANES = 128
RW_L = 64
VMEM_LIMIT = 56 * 1024 * 1024
ROW_TILE = 512


def _cp(dims, vmem=VMEM_LIMIT):
    return pltpu.CompilerParams(dimension_semantics=dims, vmem_limit_bytes=vmem)


def _tile(n, pref, mult=8):
    t = min(pref, n)
    t -= t % mult
    while t >= mult:
        if n % t == 0:
            return t
        t -= mult
    return n


def _dot(a, b):
    return jnp.dot(a.astype(BF16), b.astype(BF16), preferred_element_type=F32)


def _dot_nt(a, b):
    return lax.dot_general(a.astype(BF16), b.astype(BF16), (((1,), (1,)), ((), ())),
                           preferred_element_type=F32)


def _dot_tn(a, b):
    return lax.dot_general(a.astype(BF16), b.astype(BF16), (((0,), (0,)), ((), ())),
                           preferred_element_type=F32)


def _split2(x):
    hi = x.astype(BF16)
    lo = (x - hi.astype(F32)).astype(BF16)
    return hi, lo


def _split3(x):
    hi = x.astype(BF16)
    r1 = x - hi.astype(F32)
    mid = r1.astype(BF16)
    lo = (r1 - mid.astype(F32)).astype(BF16)
    return hi, mid, lo


def _silu(x):
    return x / (1.0 + jnp.exp(-x))


def _swap_halves(x, seg):
    n = x.shape[-1]
    lane = lax.broadcasted_iota(jnp.int32, x.shape, x.ndim - 1)
    half = seg // 2
    fwd = pltpu.roll(x, n - half, axis=x.ndim - 1)
    bwd = pltpu.roll(x, half, axis=x.ndim - 1)
    return jnp.where((lane % seg) < half, fwd, bwd)


def _seg_sum(x, bd_ref):
    bd = bd_ref[...]
    rows, n = x.shape[0], x.shape[-1] // LANES
    xs = jnp.concatenate([x[:, p * LANES:(p + 1) * LANES] for p in range(n)], axis=0)
    hi, lo = _split2(xs)
    ys = jnp.dot(hi, bd, preferred_element_type=F32) + jnp.dot(lo, bd, preferred_element_type=F32)
    return jnp.concatenate([ys[p * rows:(p + 1) * rows] for p in range(n)], axis=-1)


def _ada_kernel(c_ref, w_ref, b_ref, o_ref):
    o_ref[0] = jnp.dot(c_ref[...], w_ref[0].astype(BF16), preferred_element_type=F32) + b_ref[0]


def _ada(c_bf16, w_ada, b_ada):
    nl, d, n3 = w_ada.shape
    rows = c_bf16.shape[0]
    tn = _tile(n3, 512, LANES)
    return pl.pallas_call(
        _ada_kernel,
        grid=(nl, n3 // tn),
        in_specs=[
            pl.BlockSpec((rows, d), lambda l, j: (0, 0)),
            pl.BlockSpec((1, d, tn), lambda l, j: (l, 0, j)),
            pl.BlockSpec((1, 1, tn), lambda l, j: (l, 0, j)),
        ],
        out_specs=pl.BlockSpec((1, rows, tn), lambda l, j: (l, 0, j)),
        out_shape=jax.ShapeDtypeStruct((nl, rows, n3), F32),
        compiler_params=_cp(("arbitrary", "arbitrary")),
        name="ada",
    )(c_bf16, w_ada, b_ada.reshape(nl, 1, n3))


def _proj_kernel(x_ref, g_ref, sc_ref, sh_ref, w_ref, o_ref, h_ref):
    bb, tt, d = x_ref.shape

    @pl.when(pl.program_id(2) == 0)
    def _():
        x = x_ref[...]
        ms = jnp.mean(x * x, axis=-1, keepdims=True)
        y = x * lax.rsqrt(ms + NORM_EPS) * g_ref[...]
        h = y * (1.0 + sc_ref[...]) + sh_ref[...]
        h_ref[...] = h.reshape(bb * tt, d).astype(BF16)

    acc = jnp.dot(h_ref[...], w_ref[...], preferred_element_type=F32)
    o_ref[...] = acc.reshape(bb, tt, acc.shape[-1])


def _proj(x, norm_g, scale, shift, w_in_p, bb, tt):
    b, t, d = x.shape
    n = w_in_p.shape[1]
    tn = _tile(n, 1024, LANES)
    return pl.pallas_call(
        _proj_kernel,
        grid=(b // bb, t // tt, n // tn),
        in_specs=[
            pl.BlockSpec((bb, tt, d), lambda i, j, k: (i, j, 0)),
            pl.BlockSpec((1, 1, d), lambda i, j, k: (0, 0, 0)),
            pl.BlockSpec((bb, 1, d), lambda i, j, k: (i, 0, 0)),
            pl.BlockSpec((bb, 1, d), lambda i, j, k: (i, 0, 0)),
            pl.BlockSpec((d, tn), lambda i, j, k: (0, k)),
        ],
        out_specs=pl.BlockSpec((bb, tt, tn), lambda i, j, k: (i, j, k)),
        out_shape=jax.ShapeDtypeStruct((b, t, n), F32),
        scratch_shapes=[pltpu.VMEM((bb * tt, d), BF16)],
        compiler_params=_cp(("arbitrary", "arbitrary", "arbitrary")),
        name="proj",
    )(x, norm_g.reshape(1, 1, d), scale, shift, w_in_p)


def _rwkv_kernel(r_ref, k_ref, v_ref, la_ref, g_ref,
                 shr_ref, shk_ref, shv_ref, shla_ref, s0_ref,
                 mur_ref, muk_ref, muv_ref, mula_ref, w2a2_ref,
                 w0_ref, a0_ref, kkg_ref, kag_ref, rk_ref, lng_ref, lnb_ref, bd_ref,
                 out_ref, sout_ref,
                 s_scr, pr_scr, pk_scr, pv_scr, pla_scr, *, t_valid):
    L = RW_L
    t_idx = pl.program_id(1)
    n_pairs = s_scr.shape[0]
    rd = r_ref.shape[-1]

    @pl.when(t_idx == 0)
    def _():
        s_scr[...] = s0_ref[0]
        pr_scr[...] = shr_ref[0]
        pk_scr[...] = shk_ref[0]
        pv_scr[...] = shv_ref[0]
        pla_scr[...] = shla_ref[0]

    def token_shift(x, prev_ref, mu_ref):
        row = lax.broadcasted_iota(jnp.int32, x.shape, 0)
        prev = jnp.where(row == 0, prev_ref[...], pltpu.roll(x, 1, axis=0))
        prev_ref[...] = x[L - 1:L, :]
        return x + (prev - x) * mu_ref[...]

    xr = token_shift(r_ref[0], pr_scr, mur_ref)
    xk = token_shift(k_ref[0], pk_scr, muk_ref)
    xv = token_shift(v_ref[0], pv_scr, muv_ref)
    xla = token_shift(la_ref[0], pla_scr, mula_ref)

    lane_la = lax.broadcasted_iota(jnp.int32, xla.shape, 1)
    la_in = jnp.where(lane_la < RW_LORA, jnp.tanh(xla), xla)
    wa = jnp.dot(la_in.astype(BF16), w2a2_ref[...], preferred_element_type=F32)
    wpre = wa[:, :rd] + w0_ref[...]
    apre = wa[:, rd:] + a0_ref[...]
    nz = -wpre
    softplus = jnp.maximum(nz, 0.0) + jnp.log(1.0 + jnp.exp(-jnp.abs(nz)))
    lw = -jnp.exp(-softplus - 0.5)
    a = 1.0 / (1.0 + jnp.exp(-apre))

    kk = xk * kkg_ref[...]
    kk = kk * lax.rsqrt(_seg_sum(kk * kk, bd_ref) + 1e-12)
    kmod = xk * (1.0 + (a - 1.0) * kag_ref[...])
    bvec = kk * a

    row_full = lax.broadcasted_iota(jnp.int32, (L, rd), 0) + t_idx * L
    live = row_full < t_valid
    lw = jnp.where(live, lw, 0.0)
    kk = jnp.where(live, kk, 0.0)
    bvec = jnp.where(live, bvec, 0.0)
    kmod_s = jnp.where(live, kmod, 0.0)
    v_s = jnp.where(live, xv, 0.0)

    ri = lax.broadcasted_iota(jnp.int32, (L, L), 0)
    ci = lax.broadcasted_iota(jnp.int32, (L, L), 1)
    tri = jnp.where(ci <= ri, 1.0, 0.0).astype(BF16)
    l_hi, l_mid, l_lo = _split3(lw)
    cum = (jnp.dot(tri, l_hi, preferred_element_type=F32)
           + jnp.dot(tri, l_mid, preferred_element_type=F32)
           + jnp.dot(tri, l_lo, preferred_element_type=F32))
    tot = cum[L - 1:L, :]
    e_pos = jnp.exp(cum)
    e_neg = jnp.exp(-cum)
    e_prev = jnp.exp(cum - lw)
    e_tail = jnp.exp(tot - cum)
    e_tot = jnp.exp(tot)

    at_f = -kk * e_prev
    rt_f = xr * e_pos
    bh_f = bvec * e_neg
    kh_f = kmod_s * e_neg
    bb_f = bvec * e_tail
    kb_f = kmod_s * e_tail

    lane = lax.broadcasted_iota(jnp.int32, (L, LANES), 1)
    rowl = lax.broadcasted_iota(jnp.int32, (L, LANES), 0)
    m0 = lane < RW_HEAD
    strict2 = (lane % RW_HEAD) < rowl
    incl2 = (lane % RW_HEAD) <= rowl
    r2 = lax.broadcasted_iota(jnp.int32, (2 * L, LANES), 0)
    c2 = lax.broadcasted_iota(jnp.int32, (2 * L, LANES), 1)
    eye2 = jnp.where(r2 == c2, 1.0, 0.0)
    bdm = (r2 < RW_HEAD) == (c2 < RW_HEAD)
    zeros_l = jnp.zeros((L, LANES), F32)

    pairs = range(n_pairs)

    def tiles(x):
        return [x[:, p * LANES:(p + 1) * LANES] for p in pairs]

    at, rt, bh, kh, bb_, kb, vp = map(tiles, (at_f, rt_f, bh_f, kh_f, bb_f, kb_f, v_s))
    sblk = [s_scr[p] for p in pairs]

    def nt3(a, b):
        a_h, a_l = _split2(a)
        b_h, b_l = _split2(b)
        return _dot_nt(a_h, b_h) + _dot_nt(a_h, b_l) + _dot_nt(a_l, b_h)

    aa0 = [nt3(jnp.concatenate([jnp.where(m0, at[p], 0.0), jnp.where(m0, rt[p], 0.0)], axis=0),
               jnp.concatenate([bh[p], kh[p]], axis=0)) for p in pairs]
    aa1 = [nt3(jnp.concatenate([jnp.where(m0, 0.0, at[p]), jnp.where(m0, 0.0, rt[p])], axis=0),
               jnp.concatenate([kh[p], bh[p]], axis=0)) for p in pairs]
    x0 = [jnp.where(strict2, aa0[p][0:L], 0.0) for p in pairs]
    x1 = [jnp.where(strict2, aa1[p][0:L], 0.0) for p in pairs]
    y0 = [jnp.where(incl2, aa0[p][L:], 0.0) for p in pairs]
    y1 = [jnp.where(incl2, aa1[p][L:], 0.0) for p in pairs]
    n_bd = [jnp.concatenate([jnp.where(m0, x0[p], 0.0), jnp.where(m0, 0.0, x1[p])], axis=0)
            for p in pairs]
    ak_bd = [jnp.concatenate([jnp.where(m0, 0.0, x0[p]), jnp.where(m0, x1[p], 0.0)], axis=0)
             for p in pairs]

    sr = [_dot_nt(jnp.concatenate([at[p], rt[p]], axis=0), sblk[p]) for p in pairs]
    rhs = [_dot(ak_bd[p], jnp.concatenate([vp[p], vp[p]], axis=0))
           + jnp.concatenate([sr[p][0:L], sr[p][0:L]], axis=0) for p in pairs]

    pinv = [eye2 + n_bd[p] for p in pairs]
    npow = n_bd
    for _ in range(int(math.log2(L)) - 1):
        npow = [_dot(npow[p], npow[p]) for p in pairs]
        pinv = [pinv[p] + _dot(npow[p], pinv[p]) for p in pairs]

    z_st = [_dot(pinv[p], rhs[p]) for p in pairs]
    z = [jnp.where(m0, z_st[p][0:L], z_st[p][L:]) for p in pairs]

    o_st = [_dot(jnp.concatenate([jnp.concatenate([y0[p], zeros_l], axis=1),
                                  jnp.concatenate([zeros_l, y1[p]], axis=1)], axis=0),
                 jnp.concatenate([z[p], vp[p], vp[p], z[p]], axis=0))
            for p in pairs]
    o_tiles = [jnp.where(m0, o_st[p][0:L], o_st[p][L:]) + sr[p][L:] for p in pairs]

    upd = [_dot_tn(jnp.concatenate([z[p], vp[p]], axis=0), jnp.concatenate([bb_[p], kb[p]], axis=0))
           for p in pairs]
    for p in pairs:
        s_scr[p] = sblk[p] * e_tot[:, p * LANES:(p + 1) * LANES] + jnp.where(bdm, upd[p], 0.0)

    o = jnp.concatenate(o_tiles, axis=-1)
    inv_n = 1.0 / RW_HEAD
    mu = _seg_sum(o, bd_ref) * inv_n
    oc = o - mu
    var = _seg_sum(oc * oc, bd_ref) * inv_n
    on = oc * lax.rsqrt(var + RW_GN_EPS) * lng_ref[...] + lnb_ref[...]
    bonus = _seg_sum(xr * kmod * rk_ref[...], bd_ref) * xv
    out_ref[0] = ((on + bonus) * _silu(g_ref[0])).astype(out_ref.dtype)

    @pl.when(t_idx == pl.num_programs(1) - 1)
    def _():
        sout_ref[0] = s_scr[...]


def _rwkv(proj, offs, shift0, s0_blk, lp, t_valid):
    b, t, _ = proj.shape
    rd = lp["mu_r"].shape[-1]
    n_pairs = rd // LANES
    L = RW_L
    cr, ck, cv, cg, cla = (offs["r"] // rd, offs["k"] // rd, offs["v"] // rd,
                           offs["rw_gate"] // rd, offs["la"] // LANES)

    def col(c, w):
        return pl.BlockSpec((1, L, w), lambda i, j, c=c: (i, j, c))

    def per_b(w):
        return pl.BlockSpec((1, 1, w), lambda i, j: (i, 0, 0))

    def const(shape):
        return pl.BlockSpec(shape, lambda i, j: (0,) * len(shape))

    vec = const((1, rd))
    out, s_out = pl.pallas_call(
        functools.partial(_rwkv_kernel, t_valid=t_valid),
        grid=(b, t // L),
        in_specs=[col(cr, rd), col(ck, rd), col(cv, rd), col(cla, LANES), col(cg, rd),
                  per_b(rd), per_b(rd), per_b(rd), per_b(LANES),
                  pl.BlockSpec((1, n_pairs, LANES, LANES), lambda i, j: (i, 0, 0, 0)),
                  vec, vec, vec, const((1, LANES)), const((LANES, 2 * rd)),
                  vec, vec, vec, vec, vec, vec, vec, const((LANES, LANES))],
        out_specs=[pl.BlockSpec((1, L, rd), lambda i, j: (i, j, 0)),
                   pl.BlockSpec((1, n_pairs, LANES, LANES), lambda i, j: (i, 0, 0, 0))],
        out_shape=[jax.ShapeDtypeStruct((b, t, rd), BF16),
                   jax.ShapeDtypeStruct((b, n_pairs, LANES, LANES), F32)],
        scratch_shapes=[pltpu.VMEM((n_pairs, LANES, LANES), F32),
                        pltpu.VMEM((1, rd), F32), pltpu.VMEM((1, rd), F32),
                        pltpu.VMEM((1, rd), F32), pltpu.VMEM((1, LANES), F32)],
        compiler_params=_cp(("arbitrary", "arbitrary")),
        name="rwkv",
    )(proj, proj, proj, proj, proj,
      shift0["r"], shift0["k"], shift0["v"], shift0["la"], s0_blk,
      lp["mu_r"], lp["mu_k"], lp["mu_v"], lp["mu_la"], lp["w2a2"],
      lp["w0"], lp["a0"], lp["kk_g"], lp["ka_g"], lp["rk"], lp["ln_g"], lp["ln_b"], lp["bd"])
    return out, s_out


def _latkr_kernel(ckv_ref, kr_ref, gkv_ref, gkr_ref, cos_ref, sin_ref, lat_ref, kro_ref):
    x = ckv_ref[0]
    lat_ref[0] = x * lax.rsqrt(jnp.mean(x * x, axis=-1, keepdims=True) + NORM_EPS) * gkv_ref[...]
    y = kr_ref[0]
    ms = jnp.sum(y * y, axis=-1, keepdims=True) * (1.0 / MLA_ROPE)
    yn = y * lax.rsqrt(ms + NORM_EPS) * gkr_ref[...]
    rot = yn * cos_ref[...] + _swap_halves(yn, MLA_ROPE) * sin_ref[...]
    kro_ref[0] = rot[:, :MLA_ROPE]


def _latkr(proj, offs, lp, cos_t, sin_t, tt):
    b, t, _ = proj.shape
    kvl = lp["kvnorm_g"].shape[-1]
    return pl.pallas_call(
        _latkr_kernel,
        grid=(b, t // tt),
        in_specs=[pl.BlockSpec((1, tt, kvl), lambda i, j: (i, j, offs["ckv"] // kvl)),
                  pl.BlockSpec((1, tt, LANES), lambda i, j: (i, j, offs["kr"] // LANES)),
                  pl.BlockSpec((1, kvl), lambda i, j: (0, 0)),
                  pl.BlockSpec((1, LANES), lambda i, j: (0, 0)),
                  pl.BlockSpec((tt, LANES), lambda i, j: (j, 0)),
                  pl.BlockSpec((tt, LANES), lambda i, j: (j, 0))],
        out_specs=[pl.BlockSpec((1, tt, kvl), lambda i, j: (i, j, 0)),
                   pl.BlockSpec((1, tt, MLA_ROPE), lambda i, j: (i, j, 0))],
        out_shape=[jax.ShapeDtypeStruct((b, t, kvl), F32),
                   jax.ShapeDtypeStruct((b, t, MLA_ROPE), F32)],
        compiler_params=_cp(("arbitrary", "arbitrary")),
        name="latkr",
    )(proj, proj, lp["kvnorm_g"], lp["kn_rope_g"], cos_t, sin_t)


def _qproj_kernel(cq_ref, gq_ref, w_ref, gn_ref, gr_ref, cos_ref, sin_ref, bd_ref, q_ref):
    x = cq_ref[0]
    xn = x * lax.rsqrt(jnp.mean(x * x, axis=-1, keepdims=True) + NORM_EPS) * gq_ref[...]
    q = jnp.dot(xn.astype(BF16), w_ref[...], preferred_element_type=F32)
    n_nope = MLA_HEADS * MLA_NOPE
    qr = q[:, n_nope:]
    ms = _seg_sum(qr * qr, bd_ref) * (1.0 / MLA_ROPE)
    qr = qr * lax.rsqrt(ms + NORM_EPS) * gr_ref[...]
    reps = qr.shape[-1] // LANES
    cos = jnp.concatenate([cos_ref[...]] * reps, axis=-1)
    sin = jnp.concatenate([sin_ref[...]] * reps, axis=-1)
    qr = qr * cos + _swap_halves(qr, MLA_ROPE) * sin
    lane = lax.broadcasted_iota(jnp.int32, (x.shape[0], LANES), 1)
    for h in range(MLA_HEADS):
        sl = slice(h * MLA_NOPE, (h + 1) * MLA_NOPE)
        qh = q[:, sl]
        qh = qh * lax.rsqrt(jnp.mean(qh * qh, axis=-1, keepdims=True) + NORM_EPS) * gn_ref[:, sl]
        pair = qr[:, (h // 2) * LANES:(h // 2 + 1) * LANES]
        mine = (lane < MLA_ROPE) if h % 2 == 0 else (lane >= MLA_ROPE)
        q_ref[0, h] = jnp.concatenate([qh, jnp.where(mine, pair, 0.0)], axis=-1).astype(q_ref.dtype)


def _qproj(proj, offs, lp, cos_t, sin_t, tt):
    b, t, _ = proj.shape
    ql = lp["qnorm_g"].shape[-1]
    n_all = lp["w_uq"].shape[1]
    n_nope = MLA_HEADS * MLA_NOPE
    n_rope = MLA_HEADS * MLA_ROPE
    qk_dim = MLA_NOPE + LANES
    return pl.pallas_call(
        _qproj_kernel,
        grid=(b, t // tt),
        in_specs=[pl.BlockSpec((1, tt, ql), lambda i, j: (i, j, offs["cq"] // ql)),
                  pl.BlockSpec((1, ql), lambda i, j: (0, 0)),
                  pl.BlockSpec((ql, n_all), lambda i, j: (0, 0)),
                  pl.BlockSpec((1, n_nope), lambda i, j: (0, 0)),
                  pl.BlockSpec((1, n_rope), lambda i, j: (0, 0)),
                  pl.BlockSpec((tt, LANES), lambda i, j: (j, 0)),
                  pl.BlockSpec((tt, LANES), lambda i, j: (j, 0)),
                  pl.BlockSpec((LANES, LANES), lambda i, j: (0, 0))],
        out_specs=pl.BlockSpec((1, MLA_HEADS, tt, qk_dim), lambda i, j: (i, 0, j, 0)),
        out_shape=jax.ShapeDtypeStruct((b, MLA_HEADS, t, qk_dim), BF16),
        compiler_params=_cp(("arbitrary", "arbitrary")),
        name="qproj",
    )(proj, lp["qnorm_g"], lp["w_uq"], lp["qn_nope_g"], lp["qn_rope_g"], cos_t, sin_t, lp["bd"])


def _kv_kernel(lat_ref, kr_ref, wk_ref, wvt_ref, gk_ref, k_ref, vt_ref):
    lat = lat_ref[0].astype(BF16)
    kr = kr_ref[0]
    kf = jnp.dot(lat, wk_ref[...], preferred_element_type=F32)
    for h in range(MLA_HEADS):
        sl = slice(h * MLA_NOPE, (h + 1) * MLA_NOPE)
        kh = kf[:, sl]
        kh = kh * lax.rsqrt(jnp.mean(kh * kh, axis=-1, keepdims=True) + NORM_EPS) * gk_ref[:, sl]
        k_ref[0, h] = jnp.concatenate([kh.astype(k_ref.dtype), kr], axis=-1)
    vt = _dot_nt(wvt_ref[...], lat)
    vt_ref[0] = vt.reshape(MLA_HEADS, MLA_VHEAD, vt.shape[-1]).astype(vt_ref.dtype)


def _kv(lat_all, krd, lp, tt):
    b, tk, kvl = lat_all.shape
    nk = lp["w_uk"].shape[1]
    nv = lp["w_uv_t"].shape[0]
    qk_dim = MLA_NOPE + LANES
    return pl.pallas_call(
        _kv_kernel,
        grid=(b, tk // tt),
        in_specs=[pl.BlockSpec((1, tt, kvl), lambda i, j: (i, j, 0)),
                  pl.BlockSpec((1, tt, LANES), lambda i, j: (i, j, 0)),
                  pl.BlockSpec((kvl, nk), lambda i, j: (0, 0)),
                  pl.BlockSpec((nv, kvl), lambda i, j: (0, 0)),
                  pl.BlockSpec((1, nk), lambda i, j: (0, 0))],
        out_specs=[pl.BlockSpec((1, MLA_HEADS, tt, qk_dim), lambda i, j: (i, 0, j, 0)),
                   pl.BlockSpec((1, MLA_HEADS, MLA_VHEAD, tt), lambda i, j: (i, 0, 0, j))],
        out_shape=[jax.ShapeDtypeStruct((b, MLA_HEADS, tk, qk_dim), BF16),
                   jax.ShapeDtypeStruct((b, MLA_HEADS, MLA_VHEAD, tk), BF16)],
        compiler_params=_cp(("arbitrary", "arbitrary")),
        name="kv",
    )(lat_all, krd, lp["w_uk"], lp["w_uv_t"], lp["kn_nope_g"])


def _attn_kernel(q_ref, k_ref, vt_ref, g_ref, o_ref, acc_scr, *, past, kv_len, tq, tk):
    t = q_ref.shape[2]
    shift = CHUNK.bit_length() - 1
    for qi in range(t // tq):
        q0 = qi * tq
        qc = q_ref[0, 0, q0:q0 + tq, :]
        cq_lo = (past + q0) // CHUNK
        cq_hi = (past + q0 + tq - 1) // CHUNK
        n_full = min(((cq_lo + 1) * CHUNK) // tk, kv_len // tk)
        n_vis = min(-(-((cq_hi + 1) * CHUNK) // tk), -(-kv_len // tk))
        q_chunk = (past + q0 + lax.broadcasted_iota(jnp.int32, (1, tq), 1)) >> shift
        m = jnp.full((1, tq), -jnp.inf, F32)
        l = jnp.zeros((1, tq), F32)
        for ki in range(n_vis):
            k0 = ki * tk
            st = _dot_nt(k_ref[0, 0, k0:k0 + tk, :], qc)
            if ki >= n_full:
                k_pos = k0 + lax.broadcasted_iota(jnp.int32, (tk, 1), 0)
                vis = (k_pos >> shift) <= q_chunk
                if k0 + tk > kv_len:
                    vis = jnp.logical_and(vis, k_pos < kv_len)
                st = jnp.where(vis, st, MASK_VALUE)
            m_new = jnp.maximum(m, jnp.max(st, axis=0, keepdims=True))
            p = jnp.exp(st - m_new)
            pv = jnp.dot(vt_ref[0, 0, :, k0:k0 + tk], p.astype(BF16),
                         preferred_element_type=F32)
            if ki == 0:
                l = jnp.sum(p, axis=0, keepdims=True)
                acc_scr[...] = pv
            else:
                alpha = jnp.exp(m - m_new)
                l = alpha * l + jnp.sum(p, axis=0, keepdims=True)
                acc_scr[...] = alpha * acc_scr[...] + pv
            m = m_new
        o = (acc_scr[...] * (1.0 / l)).T
        o_ref[0, q0:q0 + tq, :] = (o * _silu(g_ref[0, q0:q0 + tq, :])).astype(o_ref.dtype)


def _attn(q, k, vt, proj, offs, past, kv_len, tq, tk):
    b, nh, t, qk_dim = q.shape
    tkp = k.shape[2]
    gate_c0 = offs["mla_gate"] // LANES
    return pl.pallas_call(
        functools.partial(_attn_kernel, past=past, kv_len=kv_len, tq=tq, tk=tk),
        grid=(b, nh),
        in_specs=[pl.BlockSpec((1, 1, t, qk_dim), lambda bi, h: (bi, h, 0, 0)),
                  pl.BlockSpec((1, 1, tkp, qk_dim), lambda bi, h: (bi, h, 0, 0)),
                  pl.BlockSpec((1, 1, MLA_VHEAD, tkp), lambda bi, h: (bi, h, 0, 0)),
                  pl.BlockSpec((1, t, MLA_VHEAD), lambda bi, h: (bi, 0, gate_c0 + h))],
        out_specs=pl.BlockSpec((1, t, MLA_VHEAD), lambda bi, h: (bi, 0, h)),
        out_shape=jax.ShapeDtypeStruct((b, t, nh * MLA_VHEAD), BF16),
        scratch_shapes=[pltpu.VMEM((MLA_VHEAD, tq), F32)],
        compiler_params=_cp(("arbitrary", "arbitrary")),
        name="attn",
    )(q, k, vt, proj)


def _conv_kernel(cb_ref, cc_ref, cx_ref, cg_ref, st_ref, w_ref, b_ref, o_ref, nc_ref, carry,
                 *, t_valid):
    tt = cc_ref.shape[1]
    j = pl.program_id(1)

    @pl.when(j == 0)
    def _():
        carry[...] = st_ref[0]

    u = cc_ref[0] * cx_ref[0]
    row = lax.broadcasted_iota(jnp.int32, u.shape, 0)
    c0, c1 = carry[0:1, :], carry[1:2, :]
    u1 = jnp.where(row == 0, c1, pltpu.roll(u, 1, axis=0))
    u2 = jnp.where(row == 0, c0, jnp.where(row == 1, c1, pltpu.roll(u, 2, axis=0)))
    y = b_ref[...] + u2 * w_ref[0:1, :] + u1 * w_ref[1:2, :] + u * w_ref[2:3, :]
    o_ref[0] = (cb_ref[0] * y * _silu(cg_ref[0])).astype(o_ref.dtype)
    carry[...] = u[tt - 2:tt, :]

    @pl.when(j == (t_valid - 1) // tt)
    def _():
        lr = (t_valid - 1) % tt
        nc_ref[0] = u[lr - 1:lr + 1, :]


def _conv(proj, offs, conv_state, lp, tt, t_valid):
    b, t, _ = proj.shape
    cd = lp["conv_b"].shape[-1]

    def col(name):
        return pl.BlockSpec((1, tt, cd), lambda i, j, c=offs[name] // cd: (i, j, c))

    return pl.pallas_call(
        functools.partial(_conv_kernel, t_valid=t_valid),
        grid=(b, t // tt),
        in_specs=[col("cv_b"), col("cv_c"), col("cv_x"), col("cv_gate"),
                  pl.BlockSpec((1, CONV_W - 1, cd), lambda i, j: (i, 0, 0)),
                  pl.BlockSpec((CONV_W, cd), lambda i, j: (0, 0)),
                  pl.BlockSpec((1, cd), lambda i, j: (0, 0))],
        out_specs=[pl.BlockSpec((1, tt, cd), lambda i, j: (i, j, 0)),
                   pl.BlockSpec((1, CONV_W - 1, cd), lambda i, j: (i, 0, 0))],
        out_shape=[jax.ShapeDtypeStruct((b, t, cd), BF16),
                   jax.ShapeDtypeStruct((b, CONV_W - 1, cd), F32)],
        scratch_shapes=[pltpu.VMEM((CONV_W - 1, cd), F32)],
        compiler_params=_cp(("arbitrary", "arbitrary")),
        name="conv",
    )(proj, proj, proj, proj, conv_state, lp["conv_w"], lp["conv_b"])


def _outproj_kernel(rw_ref, mla_ref, cv_ref, w_ref, x_ref, gt_ref, o_ref):
    bb, tt, tn = x_ref.shape
    d1, d2 = rw_ref.shape[-1], mla_ref.shape[-1]

    def flat(r):
        return r[...].reshape(bb * tt, r.shape[-1])

    acc = (jnp.dot(flat(rw_ref), w_ref[0, 0:d1, :], preferred_element_type=F32)
           + jnp.dot(flat(mla_ref), w_ref[0, d1:d1 + d2, :], preferred_element_type=F32)
           + jnp.dot(flat(cv_ref), w_ref[0, d1 + d2:, :], preferred_element_type=F32))
    o_ref[...] = x_ref[...] + gt_ref[...] * acc.reshape(bb, tt, tn)


def _outproj(rw, mla, cv, w_out_bf16, layer, x, gate, bb, tt):
    b, t, d = x.shape
    tn = _tile(d, 1024, LANES)
    d1, d2, d3 = rw.shape[-1], mla.shape[-1], cv.shape[-1]
    return pl.pallas_call(
        _outproj_kernel,
        grid=(b // bb, t // tt, d // tn),
        in_specs=[pl.BlockSpec((bb, tt, d1), lambda i, j, k: (i, j, 0)),
                  pl.BlockSpec((bb, tt, d2), lambda i, j, k: (i, j, 0)),
                  pl.BlockSpec((bb, tt, d3), lambda i, j, k: (i, j, 0)),
                  pl.BlockSpec((1, d1 + d2 + d3, tn), lambda i, j, k: (layer, 0, k)),
                  pl.BlockSpec((bb, tt, tn), lambda i, j, k: (i, j, k)),
                  pl.BlockSpec((bb, 1, tn), lambda i, j, k: (i, 0, k))],
        out_specs=pl.BlockSpec((bb, tt, tn), lambda i, j, k: (i, j, k)),
        out_shape=jax.ShapeDtypeStruct((b, t, d), F32),
        compiler_params=_cp(("arbitrary", "arbitrary", "arbitrary")),
        name="outproj",
    )(rw, mla, cv, w_out_bf16, x, gate)


def _layout(d):
    rd = d // 4
    q_lora = d // 4
    kv_lora = 512
    mla_dim = MLA_HEADS * MLA_VHEAD
    cd = d - rd - mla_dim
    shift_dim = 3 * rd + 2 * RW_LORA
    src, pos = {}, 0
    for name, w in (("rw_pre", shift_dim), ("rw_gate", rd), ("cq", q_lora), ("ckv", kv_lora),
                    ("kr", MLA_ROPE), ("mla_gate", mla_dim), ("cv_b", cd), ("cv_c", cd),
                    ("cv_x", cd), ("cv_gate", cd)):
        src[name] = (pos, w)
        pos += w
    p0 = src["rw_pre"][0]
    src["r"], src["k"], src["v"] = (p0, rd), (p0 + rd, rd), (p0 + 2 * rd, rd)
    src["la"] = (p0 + 3 * rd, 2 * RW_LORA)
    order = ("r", "k", "v", "rw_gate", "cq", "cv_b", "cv_c", "cv_x", "cv_gate", "mla_gate",
             "ckv", "la", "kr")
    offs, pos = {}, 0
    for name in order:
        offs[name] = pos
        pos += src[name][1]
    total = -(-pos // 1024) * 1024
    return src, order, offs, pos, total, dict(rd=rd, q_lora=q_lora, kv_lora=kv_lora, cd=cd,
                                             shift_dim=shift_dim, in_cols=src["cv_gate"][0] + cd)


def _prep_layer(l, src, order, used, total, dims, w):
    rd = dims["rd"]
    w_in = w["w_in"][l]
    w_in_p = jnp.zeros((w_in.shape[0], total), BF16)
    pos = 0
    for n in order:
        s0, wd = src[n]
        w_in_p = lax.dynamic_update_slice(w_in_p, w_in[:, s0:s0 + wd].astype(BF16), (0, pos))
        pos += wd

    w_uq = w["mla_w_uq"][l].reshape(-1, MLA_HEADS, MLA_NOPE + MLA_ROPE)
    w_uq_p = jnp.concatenate([w_uq[:, :, :MLA_NOPE].reshape(w_uq.shape[0], -1),
                              w_uq[:, :, MLA_NOPE:].reshape(w_uq.shape[0], -1)], axis=1).astype(BF16)
    zl = jnp.zeros((RW_LORA, rd), F32)
    w2a2 = jnp.concatenate([jnp.concatenate([w["rw_w2"][l], zl], axis=1),
                            jnp.concatenate([zl, w["rw_a2"][l]], axis=1)], axis=0).astype(BF16)
    mu = w["rw_mu"][l]
    scale = float(MLA_NOPE + MLA_ROPE) ** -0.5
    idx = jnp.arange(LANES)
    bd = ((idx[:, None] // RW_HEAD) == (idx[None, :] // RW_HEAD)).astype(BF16)
    row = lambda v: v.reshape(1, -1)
    return dict(
        w_in=w_in_p, norm_g=w["norm_g"][l],
        mu_r=row(mu[:rd]), mu_k=row(mu[rd:2 * rd]), mu_v=row(mu[2 * rd:3 * rd]), mu_la=row(mu[3 * rd:]),
        w2a2=w2a2, w0=row(w["rw_w0"][l]), a0=row(w["rw_a0"][l]), kk_g=row(w["rw_kk"][l]),
        ka_g=row(w["rw_ka"][l]), rk=row(w["rw_rk"][l]), ln_g=row(w["rw_ln_g"][l]),
        ln_b=row(w["rw_ln_b"][l]), bd=bd,
        qnorm_g=row(w["mla_qnorm_g"][l]), kvnorm_g=row(w["mla_kvnorm_g"][l]),
        w_uq=w_uq_p, w_uk=w["mla_w_uk"][l].astype(BF16), w_uv_t=w["mla_w_uv"][l].T.astype(BF16),
        qn_nope_g=row(jnp.tile(w["mla_qn_nope"][l], MLA_HEADS)) * scale,
        qn_rope_g=row(jnp.tile(w["mla_qn_rope"][l], MLA_HEADS)) * scale,
        kn_nope_g=row(jnp.tile(w["mla_kn_nope"][l], MLA_HEADS)),
        kn_rope_g=row(jnp.concatenate([w["mla_kn_rope"][l], jnp.zeros((LANES - MLA_ROPE,), F32)])),
        conv_w=w["conv_w"][l], conv_b=row(w["conv_b"][l]), w_out=w["w_out_bf16"], layer=l,
    )


def _rope_tables(past, t):
    half = MLA_ROPE // 2
    freqs = ROPE_BASE ** (-jnp.arange(half, dtype=F32) / half)
    ang = jnp.arange(past, past + t, dtype=jnp.int32).astype(F32)[:, None] * freqs[None, :]
    cos, sin = jnp.cos(ang), jnp.sin(ang)
    reps = LANES // MLA_ROPE
    return (jnp.concatenate([cos, cos] * reps, axis=1), jnp.concatenate([-sin, sin] * reps, axis=1))


def _pad_rows(x, n):
    if x.shape[1] == n:
        return x
    pad = [(0, 0)] * x.ndim
    pad[1] = (0, n - x.shape[1])
    return jnp.pad(x, pad)


def _mixer(x, t_real, past, mod, rw_state, rw_shift, conv_state, lat_past, kr_past,
           lp, offs, dims):
    b, t, d = x.shape
    rd = dims["rd"]
    shift, scale, gate = mod
    tt = _tile(t, ROW_TILE)
    bb = _tile(b, max(1, ROW_TILE // tt), 1) if tt == t else 1

    proj = _proj(x, lp["norm_g"], scale, shift, lp["w_in"], bb, tt)

    n_pairs = rd // LANES
    if rw_state is None:
        s0_blk = jnp.zeros((b, n_pairs, LANES, LANES), F32)
        shift0 = dict(r=jnp.zeros((b, 1, rd), F32), k=jnp.zeros((b, 1, rd), F32),
                      v=jnp.zeros((b, 1, rd), F32), la=jnp.zeros((b, 1, LANES), F32))
    else:
        s4 = rw_state.astype(F32).reshape(b, n_pairs, 2, RW_HEAD, RW_HEAD)
        z = jnp.zeros_like(s4[:, :, 0])
        s0_blk = jnp.concatenate([jnp.concatenate([s4[:, :, 0], z], axis=-1),
                                  jnp.concatenate([z, s4[:, :, 1]], axis=-1)], axis=-2)
        sh = rw_shift[:, None, :]
        shift0 = dict(r=sh[..., :rd], k=sh[..., rd:2 * rd], v=sh[..., 2 * rd:3 * rd],
                      la=sh[..., 3 * rd:])
    rw_out, s_blk = _rwkv(proj, offs, shift0, s0_blk, lp, t_real)
    new_state = jnp.stack([s_blk[:, :, :RW_HEAD, :RW_HEAD], s_blk[:, :, RW_HEAD:, RW_HEAD:]],
                          axis=2).reshape(b, 2 * n_pairs, RW_HEAD, RW_HEAD)
    new_shift = jnp.concatenate([proj[:, t_real - 1, offs["r"]:offs["r"] + 3 * rd],
                                 proj[:, t_real - 1, offs["la"]:offs["la"] + 2 * RW_LORA]], axis=-1)

    cos_t, sin_t = _rope_tables(past, t)
    tt_s = _tile(t, 512)
    lat_new, kr_new = _latkr(proj, offs, lp, cos_t, sin_t, tt_s)
    q_cat = _qproj(proj, offs, lp, cos_t, sin_t, tt_s)
    kv_len = past + t_real
    if past == 0 and t_real == t:
        lat_all, kr_all, tk_pad = lat_new, kr_new, t
    else:
        tk_pad = -(-kv_len // LANES) * LANES
        lat_all = _pad_rows(jnp.concatenate([lat_past.astype(BF16),
                                             lat_new[:, :t_real].astype(BF16)], axis=1), tk_pad)
        kr_all = _pad_rows(jnp.concatenate([kr_past, kr_new[:, :t_real]], axis=1), tk_pad)
    krd = jnp.concatenate([kr_all, kr_all], axis=-1).astype(BF16)
    k_cat, v_t = _kv(lat_all, krd, lp, _tile(tk_pad, 512, LANES))
    tq = _tile(t, 512)
    tk = tk_pad if t < 256 else _tile(tk_pad, 512, LANES)
    mla_out = _attn(q_cat, k_cat, v_t, proj, offs, past, kv_len, tq, tk)

    if conv_state is None:
        conv_state = jnp.zeros((b, CONV_W - 1, dims["cd"]), F32)
    cv_out, new_conv = _conv(proj, offs, conv_state, lp, tt_s, t_real)

    y = _outproj(rw_out, mla_out, cv_out, lp["w_out"], lp["layer"], x, gate, bb, tt)
    return (y, lat_new[:, :t_real], kr_new[:, :t_real], new_state, new_shift, new_conv)


def kernel(x_prompt, x_sample, c_prompt, c_sample, cache_mla_latent, cache_mla_krope, state_rwkv, state_rwkv_shift, state_conv, w_ada, b_ada, norm_g, w_in, rw_mu, rw_w0, rw_w2, rw_a0, rw_a2, rw_kk, rw_ka, rw_rk, rw_ln_g, rw_ln_b, mla_qnorm_g, mla_kvnorm_g, mla_w_uq, mla_w_uk, mla_w_uv, mla_qn_nope, mla_qn_rope, mla_kn_nope, mla_kn_rope, conv_w, conv_b, w_out):
    w = dict(norm_g=norm_g, w_in=w_in, rw_mu=rw_mu, rw_w0=rw_w0, rw_w2=rw_w2, rw_a0=rw_a0,
             rw_a2=rw_a2, rw_kk=rw_kk, rw_ka=rw_ka, rw_rk=rw_rk, rw_ln_g=rw_ln_g, rw_ln_b=rw_ln_b,
             mla_qnorm_g=mla_qnorm_g, mla_kvnorm_g=mla_kvnorm_g, mla_w_uq=mla_w_uq,
             mla_w_uk=mla_w_uk, mla_w_uv=mla_w_uv, mla_qn_nope=mla_qn_nope, mla_qn_rope=mla_qn_rope,
             mla_kn_nope=mla_kn_nope, mla_kn_rope=mla_kn_rope, conv_w=conv_w, conv_b=conv_b,
             w_out_bf16=w_out.astype(BF16))
    depth = w_in.shape[0]
    bp, tp, d = x_prompt.shape
    bs, ts, _ = x_sample.shape
    past = cache_mla_latent.shape[2]
    src, order, offs, used, total, dims = _layout(d)
    assert dims["in_cols"] == w_in.shape[2] and tp % RW_L == 0

    rows = -(-(bp + bs) // 16) * 16
    c_all = jnp.concatenate([c_prompt, c_sample, jnp.zeros((rows - bp - bs, d), F32)], axis=0)
    mod = _ada(c_all.astype(BF16), w_ada, b_ada)

    def mods(l, lo, n):
        m = mod[l, lo:lo + n][:, None, :]
        return m[..., :d], m[..., d:2 * d], m[..., 2 * d:]

    ts_pad = -(-ts // RW_L) * RW_L
    yp, ys = x_prompt, _pad_rows(x_sample, ts_pad)
    outs_p, outs_s = [], []
    for l in range(depth):
        lp = _prep_layer(l, src, order, used, total, dims, w)
        rp = _mixer(yp, tp, 0, mods(l, 0, bp), None, None, None, None, None, lp, offs, dims)
        yp = rp[0]
        outs_p.append(rp[1:])
        rs = _mixer(ys, ts, past, mods(l, bp, bs), state_rwkv[l], state_rwkv_shift[l],
                    state_conv[l], cache_mla_latent[l], cache_mla_krope[l], lp, offs, dims)
        ys = rs[0]
        outs_s.append(rs[1:])

    def stack(outs, i):
        return jnp.stack([o[i] for o in outs])

    return (yp, ys[:, :ts],
            stack(outs_p, 0), stack(outs_p, 1), stack(outs_p, 2), stack(outs_p, 3), stack(outs_p, 4),
            stack(outs_s, 0), stack(outs_s, 1), stack(outs_s, 2), stack(outs_s, 3), stack(outs_s, 4))
```

```python
import functools
import math

import jax
import jax.numpy as jnp
from jax import lax
from jax.experimental import pallas as pl
from jax.experimental.pallas import tpu as pltpu

F32 = jnp.float32
BF16 = jnp.bfloat16

CHUNK = 64
RW_HEAD = 64
RW_LORA = 64
RW_GN_EPS = 64e-5
MLA_HEADS = 16
MLA_NOPE = 128
MLA_ROPE = 64
MLA_VHEAD = 128
ROPE_BASE = 10000.0
CONV_W = 3
NORM_EPS = 1e-6
MASK_VALUE = -1e30

LANES = 128
RW_L = 64
VMEM_LIMIT = 56 * 1024 * 1024
ROW_TILE = 512


def _cp(dims, vmem=VMEM_LIMIT):
    return pltpu.CompilerParams(dimension_semantics=dims, vmem_limit_bytes=vmem)


def _tile(n, pref, mult=8):
    t = min(pref, n)
    t -= t % mult
    while t >= mult:
        if n % t == 0:
            return t
        t -= mult
    return n


def _dot(a, b):
    return jnp.dot(a.astype(BF16), b.astype(BF16), preferred_element_type=F32)


def _dot_nt(a, b):
    return lax.dot_general(a.astype(BF16), b.astype(BF16), (((1,), (1,)), ((), ())),
                           preferred_element_type=F32)


def _dot_tn(a, b):
    return lax.dot_general(a.astype(BF16), b.astype(BF16), (((0,), (0,)), ((), ())),
                           preferred_element_type=F32)


def _split2(x):
    hi = x.astype(BF16)
    lo = (x - hi.astype(F32)).astype(BF16)
    return hi, lo


def _split3(x):
    hi = x.astype(BF16)
    r1 = x - hi.astype(F32)
    mid = r1.astype(BF16)
    lo = (r1 - mid.astype(F32)).astype(BF16)
    return hi, mid, lo


def _silu(x):
    return x / (1.0 + jnp.exp(-x))


def _swap_halves(x, seg):
    n = x.shape[-1]
    lane = lax.broadcasted_iota(jnp.int32, x.shape, x.ndim - 1)
    half = seg // 2
    fwd = pltpu.roll(x, n - half, axis=x.ndim - 1)
    bwd = pltpu.roll(x, half, axis=x.ndim - 1)
    return jnp.where((lane % seg) < half, fwd, bwd)


def _seg_sum(x, bd_ref, two_pass=True):
    bd = bd_ref[...]
    rows, n = x.shape[0], x.shape[-1] // LANES
    xs = jnp.concatenate([x[:, p * LANES:(p + 1) * LANES] for p in range(n)], axis=0)
    if two_pass:
        hi, lo = _split2(xs)
        ys = (jnp.dot(hi, bd, preferred_element_type=F32)
              + jnp.dot(lo, bd, preferred_element_type=F32))
    else:
        ys = jnp.dot(xs.astype(BF16), bd, preferred_element_type=F32)
    return jnp.concatenate([ys[p * rows:(p + 1) * rows] for p in range(n)], axis=-1)


def _ada_kernel(c_ref, w_ref, b_ref, o_ref):
    o_ref[0] = jnp.dot(c_ref[...], w_ref[0].astype(BF16), preferred_element_type=F32) + b_ref[0]


def _ada(c_bf16, w_ada, b_ada):
    nl, d, n3 = w_ada.shape
    rows = c_bf16.shape[0]
    tn = _tile(n3, 512, LANES)
    return pl.pallas_call(
        _ada_kernel,
        grid=(nl, n3 // tn),
        in_specs=[
            pl.BlockSpec((rows, d), lambda l, j: (0, 0)),
            pl.BlockSpec((1, d, tn), lambda l, j: (l, 0, j)),
            pl.BlockSpec((1, 1, tn), lambda l, j: (l, 0, j)),
        ],
        out_specs=pl.BlockSpec((1, rows, tn), lambda l, j: (l, 0, j)),
        out_shape=jax.ShapeDtypeStruct((nl, rows, n3), F32),
        compiler_params=_cp(("arbitrary", "arbitrary")),
        name="ada",
    )(c_bf16, w_ada, b_ada.reshape(nl, 1, n3))


WP_COPY, WP_SHIFT, WP_HALF, WP_ZERO = 0, 1, 2, 3


def _winprep_kernel(src_tbl, mode_tbl, w_ref, o_ref, carry):
    mode = mode_tbl[pl.program_id(1)]
    lane = lax.broadcasted_iota(jnp.int32, w_ref.shape[1:], 1)
    half = LANES // 2

    @pl.when(mode == WP_COPY)
    def _():
        o_ref[0] = w_ref[0].astype(o_ref.dtype)

    @pl.when(mode == WP_SHIFT)
    def _():
        x = w_ref[0]
        y = jnp.where(lane < half, pltpu.roll(carry[...], half, axis=1), pltpu.roll(x, half, axis=1))
        o_ref[0] = y.astype(o_ref.dtype)
        carry[...] = x

    @pl.when(mode == WP_HALF)
    def _():
        x = w_ref[0]
        o_ref[0] = jnp.where(lane < half, x, 0.0).astype(o_ref.dtype)
        carry[...] = x

    @pl.when(mode == WP_ZERO)
    def _():
        o_ref[0] = jnp.zeros(o_ref.shape[1:], o_ref.dtype)


def _winprep(w_in, tables):
    nl, d, _ = w_in.shape
    src_tbl, mode_tbl = (jnp.asarray(t, jnp.int32) for t in tables)
    n_tiles = len(tables[0])
    return pl.pallas_call(
        _winprep_kernel,
        grid_spec=pltpu.PrefetchScalarGridSpec(
            num_scalar_prefetch=2,
            grid=(nl, n_tiles),
            in_specs=[pl.BlockSpec((1, d, LANES), lambda l, j, st, mt: (l, 0, st[j]))],
            out_specs=pl.BlockSpec((1, d, LANES), lambda l, j, st, mt: (l, 0, j)),
            scratch_shapes=[pltpu.VMEM((d, LANES), F32)]),
        out_shape=jax.ShapeDtypeStruct((nl, d, n_tiles * LANES), BF16),
        compiler_params=_cp(("arbitrary", "arbitrary")),
        name="winprep",
    )(src_tbl, mode_tbl, w_in)


def _proj_kernel(x_ref, g_ref, sc_ref, sh_ref, w_ref, o_ref, h_ref):
    bb, tt, d = x_ref.shape

    @pl.when(pl.program_id(2) == 0)
    def _():
        x = x_ref[...]
        ms = jnp.mean(x * x, axis=-1, keepdims=True)
        y = x * lax.rsqrt(ms + NORM_EPS) * g_ref[...]
        h = y * (1.0 + sc_ref[...]) + sh_ref[...]
        h_ref[...] = h.reshape(bb * tt, d).astype(BF16)

    acc = jnp.dot(h_ref[...], w_ref[0], preferred_element_type=F32)
    o_ref[...] = acc.reshape(bb, tt, acc.shape[-1])


def _proj(x, norm_g, scale, shift, w_in_p, layer, bb, tt):
    b, t, d = x.shape
    n = w_in_p.shape[2]
    tn = _tile(n, 1024, LANES)
    return pl.pallas_call(
        _proj_kernel,
        grid=(b // bb, t // tt, n // tn),
        in_specs=[
            pl.BlockSpec((bb, tt, d), lambda i, j, k: (i, j, 0)),
            pl.BlockSpec((1, 1, d), lambda i, j, k: (0, 0, 0)),
            pl.BlockSpec((bb, 1, d), lambda i, j, k: (i, 0, 0)),
            pl.BlockSpec((bb, 1, d), lambda i, j, k: (i, 0, 0)),
            pl.BlockSpec((1, d, tn), lambda i, j, k: (layer, 0, k)),
        ],
        out_specs=pl.BlockSpec((bb, tt, tn), lambda i, j, k: (i, j, k)),
        out_shape=jax.ShapeDtypeStruct((b, t, n), F32),
        scratch_shapes=[pltpu.VMEM((bb * tt, d), BF16)],
        compiler_params=_cp(("arbitrary", "arbitrary", "arbitrary")),
        name="proj",
    )(x, norm_g.reshape(1, 1, d), scale, shift, w_in_p)


def _rwkv_kernel(r_ref, k_ref, v_ref, la_ref, g_ref,
                 shr_ref, shk_ref, shv_ref, shla_ref, s0_ref,
                 mur_ref, muk_ref, muv_ref, mula_ref, w2a2_ref,
                 w0_ref, a0_ref, kkg_ref, kag_ref, rk_ref, lng_ref, lnb_ref, bd_ref,
                 out_ref, sout_ref,
                 s_scr, pr_scr, pk_scr, pv_scr, pla_scr, *, t_valid):
    L = RW_L
    t_idx = pl.program_id(1)
    n_pairs = s_scr.shape[0]
    rd = r_ref.shape[-1]

    @pl.when(t_idx == 0)
    def _():
        s_scr[...] = s0_ref[0]
        pr_scr[...] = shr_ref[0]
        pk_scr[...] = shk_ref[0]
        pv_scr[...] = shv_ref[0]
        pla_scr[...] = shla_ref[0]

    def token_shift(x, prev_ref, mu_ref):
        row = lax.broadcasted_iota(jnp.int32, x.shape, 0)
        prev = jnp.where(row == 0, prev_ref[...], pltpu.roll(x, 1, axis=0))
        prev_ref[...] = x[L - 1:L, :]
        return x + (prev - x) * mu_ref[...]

    xr = token_shift(r_ref[0], pr_scr, mur_ref)
    xk = token_shift(k_ref[0], pk_scr, muk_ref)
    xv = token_shift(v_ref[0], pv_scr, muv_ref)
    xla = token_shift(la_ref[0], pla_scr, mula_ref)

    lane_la = lax.broadcasted_iota(jnp.int32, xla.shape, 1)
    la_in = jnp.where(lane_la < RW_LORA, jnp.tanh(xla), xla)
    wa = jnp.dot(la_in.astype(BF16), w2a2_ref[...], preferred_element_type=F32)
    wpre = wa[:, :rd] + w0_ref[...]
    apre = wa[:, rd:] + a0_ref[...]
    nz = -wpre
    softplus = jnp.maximum(nz, 0.0) + jnp.log(1.0 + jnp.exp(-jnp.abs(nz)))
    lw = -jnp.exp(-softplus - 0.5)
    a = 1.0 / (1.0 + jnp.exp(-apre))

    kk = xk * kkg_ref[...]
    kk = kk * lax.rsqrt(_seg_sum(kk * kk, bd_ref, two_pass=False) + 1e-12)
    kmod = xk * (1.0 + (a - 1.0) * kag_ref[...])
    bvec = kk * a

    row_full = lax.broadcasted_iota(jnp.int32, (L, rd), 0) + t_idx * L
    live = row_full < t_valid
    lw = jnp.where(live, lw, 0.0)
    kk = jnp.where(live, kk, 0.0)
    bvec = jnp.where(live, bvec, 0.0)
    kmod_s = jnp.where(live, kmod, 0.0)
    v_s = jnp.where(live, xv, 0.0)

    ri = lax.broadcasted_iota(jnp.int32, (L, L), 0)
    ci = lax.broadcasted_iota(jnp.int32, (L, L), 1)
    tri = jnp.where(ci <= ri, 1.0, 0.0).astype(BF16)
    l_hi, l_mid, l_lo = _split3(lw)
    cum = (jnp.dot(tri, l_hi, preferred_element_type=F32)
           + jnp.dot(tri, l_mid, preferred_element_type=F32)
           + jnp.dot(tri, l_lo, preferred_element_type=F32))
    tot = cum[L - 1:L, :]
    e_pos = jnp.exp(cum)
    e_neg = jnp.exp(-cum)
    e_prev = jnp.exp(cum - lw)
    e_tail = jnp.exp(tot - cum)
    e_tot = jnp.exp(tot)

    at_f = -kk * e_prev
    rt_f = xr * e_pos
    bh_f = bvec * e_neg
    kh_f = kmod_s * e_neg
    bb_f = bvec * e_tail
    kb_f = kmod_s * e_tail

    lane = lax.broadcasted_iota(jnp.int32, (L, LANES), 1)
    rowl = lax.broadcasted_iota(jnp.int32, (L, LANES), 0)
    m0 = lane < RW_HEAD
    strict2 = (lane % RW_HEAD) < rowl
    incl2 = (lane % RW_HEAD) <= rowl
    r2 = lax.broadcasted_iota(jnp.int32, (2 * L, LANES), 0)
    c2 = lax.broadcasted_iota(jnp.int32, (2 * L, LANES), 1)
    eye2 = jnp.where(r2 == c2, 1.0, 0.0)
    bdm = (r2 < RW_HEAD) == (c2 < RW_HEAD)
    zeros_l = jnp.zeros((L, LANES), F32)

    pairs = range(n_pairs)

    def tiles(x):
        return [x[:, p * LANES:(p + 1) * LANES] for p in pairs]

    at, rt, bh, kh, bb_, kb, vp = map(tiles, (at_f, rt_f, bh_f, kh_f, bb_f, kb_f, v_s))
    sblk = [s_scr[p] for p in pairs]

    aa0 = [_dot_nt(jnp.concatenate([jnp.where(m0, at[p], 0.0), jnp.where(m0, rt[p], 0.0)], axis=0),
                   jnp.concatenate([bh[p], kh[p]], axis=0)) for p in pairs]
    aa1 = [_dot_nt(jnp.concatenate([jnp.where(m0, 0.0, at[p]), jnp.where(m0, 0.0, rt[p])], axis=0),
                   jnp.concatenate([kh[p], bh[p]], axis=0)) for p in pairs]
    x0 = [jnp.where(strict2, aa0[p][0:L], 0.0) for p in pairs]
    x1 = [jnp.where(strict2, aa1[p][0:L], 0.0) for p in pairs]
    y0 = [jnp.where(incl2, aa0[p][L:], 0.0) for p in pairs]
    y1 = [jnp.where(incl2, aa1[p][L:], 0.0) for p in pairs]
    n_bd = [jnp.concatenate([jnp.where(m0, x0[p], 0.0), jnp.where(m0, 0.0, x1[p])], axis=0)
            for p in pairs]
    ak_bd = [jnp.concatenate([jnp.where(m0, 0.0, x0[p]), jnp.where(m0, x1[p], 0.0)], axis=0)
             for p in pairs]

    sr = [_dot_nt(jnp.concatenate([at[p], rt[p]], axis=0), sblk[p]) for p in pairs]
    rhs = [_dot(ak_bd[p], jnp.concatenate([vp[p], vp[p]], axis=0))
           + jnp.concatenate([sr[p][0:L], sr[p][0:L]], axis=0) for p in pairs]

    pinv = [eye2 + n_bd[p] for p in pairs]
    npow = n_bd
    for _ in range(int(math.log2(L)) - 1):
        npow = [_dot(npow[p], npow[p]) for p in pairs]
        pinv = [pinv[p] + _dot(npow[p], pinv[p]) for p in pairs]

    z_st = [_dot(pinv[p], rhs[p]) for p in pairs]
    z = [jnp.where(m0, z_st[p][0:L], z_st[p][L:]) for p in pairs]

    o_st = [_dot(jnp.concatenate([jnp.concatenate([y0[p], zeros_l], axis=1),
                                  jnp.concatenate([zeros_l, y1[p]], axis=1)], axis=0),
                 jnp.concatenate([z[p], vp[p], vp[p], z[p]], axis=0))
            for p in pairs]
    o_tiles = [jnp.where(m0, o_st[p][0:L], o_st[p][L:]) + sr[p][L:] for p in pairs]

    upd = [_dot_tn(jnp.concatenate([z[p], vp[p]], axis=0), jnp.concatenate([bb_[p], kb[p]], axis=0))
           for p in pairs]
    for p in pairs:
        s_scr[p] = sblk[p] * e_tot[:, p * LANES:(p + 1) * LANES] + jnp.where(bdm, upd[p], 0.0)

    o = jnp.concatenate(o_tiles, axis=-1)
    inv_n = 1.0 / RW_HEAD
    mu = _seg_sum(o, bd_ref, two_pass=False) * inv_n
    oc = o - mu
    var = _seg_sum(oc * oc, bd_ref, two_pass=False) * inv_n
    on = oc * lax.rsqrt(var + RW_GN_EPS) * lng_ref[...] + lnb_ref[...]
    bonus = _seg_sum(xr * kmod * rk_ref[...], bd_ref, two_pass=False) * xv
    out_ref[0] = ((on + bonus) * _silu(g_ref[0])).astype(out_ref.dtype)

    @pl.when(t_idx == pl.num_programs(1) - 1)
    def _():
        sout_ref[0] = s_scr[...]


def _rwkv(proj, offs, shift0, s0_blk, lp, t_valid):
    b, t, _ = proj.shape
    rd = lp["mu_r"].shape[-1]
    n_pairs = rd // LANES
    L = RW_L
    cr, ck, cv, cg, cla = (offs["r"] // rd, offs["k"] // rd, offs["v"] // rd,
                           offs["rw_gate"] // rd, offs["la"] // LANES)

    def col(c, w):
        return pl.BlockSpec((1, L, w), lambda i, j, c=c: (i, j, c))

    def per_b(w):
        return pl.BlockSpec((1, 1, w), lambda i, j: (i, 0, 0))

    def const(shape):
        return pl.BlockSpec(shape, lambda i, j: (0,) * len(shape))

    vec = const((1, rd))
    out, s_out = pl.pallas_call(
        functools.partial(_rwkv_kernel, t_valid=t_valid),
        grid=(b, t // L),
        in_specs=[col(cr, rd), col(ck, rd), col(cv, rd), col(cla, LANES), col(cg, rd),
                  per_b(rd), per_b(rd), per_b(rd), per_b(LANES),
                  pl.BlockSpec((1, n_pairs, LANES, LANES), lambda i, j: (i, 0, 0, 0)),
                  vec, vec, vec, const((1, LANES)), const((LANES, 2 * rd)),
                  vec, vec, vec, vec, vec, vec, vec, const((LANES, LANES))],
        out_specs=[pl.BlockSpec((1, L, rd), lambda i, j: (i, j, 0)),
                   pl.BlockSpec((1, n_pairs, LANES, LANES), lambda i, j: (i, 0, 0, 0))],
        out_shape=[jax.ShapeDtypeStruct((b, t, rd), BF16),
                   jax.ShapeDtypeStruct((b, n_pairs, LANES, LANES), F32)],
        scratch_shapes=[pltpu.VMEM((n_pairs, LANES, LANES), F32),
                        pltpu.VMEM((1, rd), F32), pltpu.VMEM((1, rd), F32),
                        pltpu.VMEM((1, rd), F32), pltpu.VMEM((1, LANES), F32)],
        compiler_params=_cp(("arbitrary", "arbitrary")),
        name="rwkv",
    )(proj, proj, proj, proj, proj,
      shift0["r"], shift0["k"], shift0["v"], shift0["la"], s0_blk,
      lp["mu_r"], lp["mu_k"], lp["mu_v"], lp["mu_la"], lp["w2a2"],
      lp["w0"], lp["a0"], lp["kk_g"], lp["ka_g"], lp["rk"], lp["ln_g"], lp["ln_b"], lp["bd"])
    return out, s_out


def _latkr_kernel(ckv_ref, kr_ref, gkv_ref, gkr_ref, cos_ref, sin_ref, lat_ref, kro_ref):
    x = ckv_ref[0]
    lat_ref[0] = x * lax.rsqrt(jnp.mean(x * x, axis=-1, keepdims=True) + NORM_EPS) * gkv_ref[...]
    y = kr_ref[0]
    ms = jnp.sum(y * y, axis=-1, keepdims=True) * (1.0 / MLA_ROPE)
    yn = y * lax.rsqrt(ms + NORM_EPS) * gkr_ref[...]
    rot = yn * cos_ref[...] + _swap_halves(yn, MLA_ROPE) * sin_ref[...]
    kro_ref[0] = rot[:, :MLA_ROPE]


def _latkr(proj, offs, lp, cos_t, sin_t, tt):
    b, t, _ = proj.shape
    kvl = lp["kvnorm_g"].shape[-1]
    return pl.pallas_call(
        _latkr_kernel,
        grid=(b, t // tt),
        in_specs=[pl.BlockSpec((1, tt, kvl), lambda i, j: (i, j, offs["ckv"] // kvl)),
                  pl.BlockSpec((1, tt, LANES), lambda i, j: (i, j, offs["kr"] // LANES)),
                  pl.BlockSpec((1, kvl), lambda i, j: (0, 0)),
                  pl.BlockSpec((1, LANES), lambda i, j: (0, 0)),
                  pl.BlockSpec((tt, LANES), lambda i, j: (j, 0)),
                  pl.BlockSpec((tt, LANES), lambda i, j: (j, 0))],
        out_specs=[pl.BlockSpec((1, tt, kvl), lambda i, j: (i, j, 0)),
                   pl.BlockSpec((1, tt, MLA_ROPE), lambda i, j: (i, j, 0))],
        out_shape=[jax.ShapeDtypeStruct((b, t, kvl), F32),
                   jax.ShapeDtypeStruct((b, t, MLA_ROPE), F32)],
        compiler_params=_cp(("arbitrary", "arbitrary")),
        name="latkr",
    )(proj, proj, lp["kvnorm_g"], lp["kn_rope_g"], cos_t, sin_t)


def _qproj_kernel(cq_ref, gq_ref, w_ref, gn_ref, gr_ref, cos_ref, sin_ref, bd_ref, q_ref):
    x = cq_ref[0]
    xn = x * lax.rsqrt(jnp.mean(x * x, axis=-1, keepdims=True) + NORM_EPS) * gq_ref[...]
    q = jnp.dot(xn.astype(BF16), w_ref[...], preferred_element_type=F32)
    n_nope = MLA_HEADS * MLA_NOPE
    qr = q[:, n_nope:]
    ms = _seg_sum(qr * qr, bd_ref) * (1.0 / MLA_ROPE)
    qr = qr * lax.rsqrt(ms + NORM_EPS) * gr_ref[...]
    reps = qr.shape[-1] // LANES
    cos = jnp.concatenate([cos_ref[...]] * reps, axis=-1)
    sin = jnp.concatenate([sin_ref[...]] * reps, axis=-1)
    qr = qr * cos + _swap_halves(qr, MLA_ROPE) * sin
    lane = lax.broadcasted_iota(jnp.int32, (x.shape[0], LANES), 1)
    for h in range(MLA_HEADS):
        sl = slice(h * MLA_NOPE, (h + 1) * MLA_NOPE)
        qh = q[:, sl]
        qh = qh * lax.rsqrt(jnp.mean(qh * qh, axis=-1, keepdims=True) + NORM_EPS) * gn_ref[:, sl]
        pair = qr[:, (h // 2) * LANES:(h // 2 + 1) * LANES]
        mine = (lane < MLA_ROPE) if h % 2 == 0 else (lane >= MLA_ROPE)
        q_ref[0, h] = jnp.concatenate([qh, jnp.where(mine, pair, 0.0)], axis=-1).astype(q_ref.dtype)


def _qproj(proj, offs, lp, cos_t, sin_t, tt):
    b, t, _ = proj.shape
    ql = lp["qnorm_g"].shape[-1]
    n_all = lp["w_uq"].shape[1]
    n_nope = MLA_HEADS * MLA_NOPE
    n_rope = MLA_HEADS * MLA_ROPE
    qk_dim = MLA_NOPE + LANES
    return pl.pallas_call(
        _qproj_kernel,
        grid=(b, t // tt),
        in_specs=[pl.BlockSpec((1, tt, ql), lambda i, j: (i, j, offs["cq"] // ql)),
                  pl.BlockSpec((1, ql), lambda i, j: (0, 0)),
                  pl.BlockSpec((ql, n_all), lambda i, j: (0, 0)),
                  pl.BlockSpec((1, n_nope), lambda i, j: (0, 0)),
                  pl.BlockSpec((1, n_rope), lambda i, j: (0, 0)),
                  pl.BlockSpec((tt, LANES), lambda i, j: (j, 0)),
                  pl.BlockSpec((tt, LANES), lambda i, j: (j, 0)),
                  pl.BlockSpec((LANES, LANES), lambda i, j: (0, 0))],
        out_specs=pl.BlockSpec((1, MLA_HEADS, tt, qk_dim), lambda i, j: (i, 0, j, 0)),
        out_shape=jax.ShapeDtypeStruct((b, MLA_HEADS, t, qk_dim), BF16),
        compiler_params=_cp(("arbitrary", "arbitrary")),
        name="qproj",
    )(proj, lp["qnorm_g"], lp["w_uq"], lp["qn_nope_g"], lp["qn_rope_g"], cos_t, sin_t, lp["bd"])


def _kv_kernel(lat_ref, kr_ref, wk_ref, wvt_ref, gk_ref, k_ref, vt_ref):
    hg = k_ref.shape[1]
    lat = lat_ref[0].astype(BF16)
    kr = kr_ref[0]
    kf = jnp.dot(lat, wk_ref[...], preferred_element_type=F32)
    for h in range(hg):
        sl = slice(h * MLA_NOPE, (h + 1) * MLA_NOPE)
        kh = kf[:, sl]
        kh = kh * lax.rsqrt(jnp.mean(kh * kh, axis=-1, keepdims=True) + NORM_EPS) * gk_ref[:, sl]
        k_ref[0, h] = jnp.concatenate([kh.astype(k_ref.dtype), kr], axis=-1)
    vt = _dot_nt(wvt_ref[...], lat)
    vt_ref[0] = vt.reshape(hg, MLA_VHEAD, vt.shape[-1]).astype(vt_ref.dtype)


def _kv(lat_all, krd, lp, tt, hg):
    b, tk, kvl = lat_all.shape
    qk_dim = MLA_NOPE + LANES
    return pl.pallas_call(
        _kv_kernel,
        grid=(b, tk // tt, MLA_HEADS // hg),
        in_specs=[pl.BlockSpec((1, tt, kvl), lambda i, j, g: (i, j, 0)),
                  pl.BlockSpec((1, tt, LANES), lambda i, j, g: (i, j, 0)),
                  pl.BlockSpec((kvl, hg * MLA_NOPE), lambda i, j, g: (0, g)),
                  pl.BlockSpec((hg * MLA_VHEAD, kvl), lambda i, j, g: (g, 0)),
                  pl.BlockSpec((1, hg * MLA_NOPE), lambda i, j, g: (0, g))],
        out_specs=[pl.BlockSpec((1, hg, tt, qk_dim), lambda i, j, g: (i, g, j, 0)),
                   pl.BlockSpec((1, hg, MLA_VHEAD, tt), lambda i, j, g: (i, g, 0, j))],
        out_shape=[jax.ShapeDtypeStruct((b, MLA_HEADS, tk, qk_dim), BF16),
                   jax.ShapeDtypeStruct((b, MLA_HEADS, MLA_VHEAD, tk), BF16)],
        compiler_params=_cp(("arbitrary", "arbitrary", "arbitrary")),
        name="kv",
    )(lat_all, krd, lp["w_uk"], lp["w_uv_t"], lp["kn_nope_g"])


def _attn_kernel(q_ref, k_ref, vt_ref, g_ref, o_ref, acc_scr, *, past, kv_len, tq, tk):
    t = q_ref.shape[2]
    shift = CHUNK.bit_length() - 1
    for qi in range(t // tq):
        q0 = qi * tq
        qc = q_ref[0, 0, q0:q0 + tq, :]
        cq_lo = (past + q0) // CHUNK
        cq_hi = (past + q0 + tq - 1) // CHUNK
        n_full = min(((cq_lo + 1) * CHUNK) // tk, kv_len // tk)
        n_vis = min(-(-((cq_hi + 1) * CHUNK) // tk), -(-kv_len // tk))
        q_chunk = (past + q0 + lax.broadcasted_iota(jnp.int32, (1, tq), 1)) >> shift
        m = jnp.full((1, tq), -jnp.inf, F32)
        l = jnp.zeros((1, tq), F32)
        for ki in range(n_vis):
            k0 = ki * tk
            st = _dot_nt(k_ref[0, 0, k0:k0 + tk, :], qc)
            if ki >= n_full:
                k_pos = k0 + lax.broadcasted_iota(jnp.int32, (tk, 1), 0)
                vis = (k_pos >> shift) <= q_chunk
                if k0 + tk > kv_len:
                    vis = jnp.logical_and(vis, k_pos < kv_len)
                st = jnp.where(vis, st, MASK_VALUE)
            m_new = jnp.maximum(m, jnp.max(st, axis=0, keepdims=True))
            p = jnp.exp2(st - m_new)
            pv = jnp.dot(vt_ref[0, 0, :, k0:k0 + tk], p.astype(BF16),
                         preferred_element_type=F32)
            if ki == 0:
                l = jnp.sum(p, axis=0, keepdims=True)
                acc_scr[...] = pv
            else:
                alpha = jnp.exp2(m - m_new)
                l = alpha * l + jnp.sum(p, axis=0, keepdims=True)
                acc_scr[...] = alpha * acc_scr[...] + pv
            m = m_new
        o = (acc_scr[...] * (1.0 / l)).T
        o_ref[0, q0:q0 + tq, :] = (o * _silu(g_ref[0, q0:q0 + tq, :])).astype(o_ref.dtype)


def _attn(q, k, vt, proj, offs, past, kv_len, tq, tk):
    b, nh, t, qk_dim = q.shape
    tkp = k.shape[2]
    gate_c0 = offs["mla_gate"] // LANES
    return pl.pallas_call(
        functools.partial(_attn_kernel, past=past, kv_len=kv_len, tq=tq, tk=tk),
        grid=(b, nh),
        in_specs=[pl.BlockSpec((1, 1, t, qk_dim), lambda bi, h: (bi, h, 0, 0)),
                  pl.BlockSpec((1, 1, tkp, qk_dim), lambda bi, h: (bi, h, 0, 0)),
                  pl.BlockSpec((1, 1, MLA_VHEAD, tkp), lambda bi, h: (bi, h, 0, 0)),
                  pl.BlockSpec((1, t, MLA_VHEAD), lambda bi, h: (bi, 0, gate_c0 + h))],
        out_specs=pl.BlockSpec((1, t, MLA_VHEAD), lambda bi, h: (bi, 0, h)),
        out_shape=jax.ShapeDtypeStruct((b, t, nh * MLA_VHEAD), BF16),
        scratch_shapes=[pltpu.VMEM((MLA_VHEAD, tq), F32)],
        compiler_params=_cp(("arbitrary", "arbitrary")),
        name="attn",
    )(q, k, vt, proj)


def _conv_kernel(cb_ref, cc_ref, cx_ref, cg_ref, st_ref, w_ref, b_ref, o_ref, nc_ref, carry,
                 *, t_valid):
    tt = cc_ref.shape[1]
    j = pl.program_id(1)

    @pl.when(j == 0)
    def _():
        carry[...] = st_ref[0]

    u = cc_ref[0] * cx_ref[0]
    row = lax.broadcasted_iota(jnp.int32, u.shape, 0)
    c0, c1 = carry[0:1, :], carry[1:2, :]
    u1 = jnp.where(row == 0, c1, pltpu.roll(u, 1, axis=0))
    u2 = jnp.where(row == 0, c0, jnp.where(row == 1, c1, pltpu.roll(u, 2, axis=0)))
    y = b_ref[...] + u2 * w_ref[0:1, :] + u1 * w_ref[1:2, :] + u * w_ref[2:3, :]
    o_ref[0] = (cb_ref[0] * y * _silu(cg_ref[0])).astype(o_ref.dtype)
    carry[...] = u[tt - 2:tt, :]

    @pl.when(j == (t_valid - 1) // tt)
    def _():
        lr = (t_valid - 1) % tt
        nc_ref[0] = u[lr - 1:lr + 1, :]


def _conv(proj, offs, conv_state, lp, tt, t_valid):
    b, t, _ = proj.shape
    cd = lp["conv_b"].shape[-1]

    def col(name):
        return pl.BlockSpec((1, tt, cd), lambda i, j, c=offs[name] // cd: (i, j, c))

    return pl.pallas_call(
        functools.partial(_conv_kernel, t_valid=t_valid),
        grid=(b, t // tt),
        in_specs=[col("cv_b"), col("cv_c"), col("cv_x"), col("cv_gate"),
                  pl.BlockSpec((1, CONV_W - 1, cd), lambda i, j: (i, 0, 0)),
                  pl.BlockSpec((CONV_W, cd), lambda i, j: (0, 0)),
                  pl.BlockSpec((1, cd), lambda i, j: (0, 0))],
        out_specs=[pl.BlockSpec((1, tt, cd), lambda i, j: (i, j, 0)),
                   pl.BlockSpec((1, CONV_W - 1, cd), lambda i, j: (i, 0, 0))],
        out_shape=[jax.ShapeDtypeStruct((b, t, cd), BF16),
                   jax.ShapeDtypeStruct((b, CONV_W - 1, cd), F32)],
        scratch_shapes=[pltpu.VMEM((CONV_W - 1, cd), F32)],
        compiler_params=_cp(("arbitrary", "arbitrary")),
        name="conv",
    )(proj, proj, proj, proj, conv_state, lp["conv_w"], lp["conv_b"])


def _outproj_kernel(rw_ref, mla_ref, cv_ref, w_ref, x_ref, gt_ref, o_ref):
    bb, tt, tn = x_ref.shape
    d1, d2 = rw_ref.shape[-1], mla_ref.shape[-1]

    def flat(r):
        return r[...].reshape(bb * tt, r.shape[-1])

    acc = (jnp.dot(flat(rw_ref), w_ref[0, 0:d1, :], preferred_element_type=F32)
           + jnp.dot(flat(mla_ref), w_ref[0, d1:d1 + d2, :], preferred_element_type=F32)
           + jnp.dot(flat(cv_ref), w_ref[0, d1 + d2:, :], preferred_element_type=F32))
    o_ref[...] = x_ref[...] + gt_ref[...] * acc.reshape(bb, tt, tn)


def _outproj(rw, mla, cv, w_out_bf16, layer, x, gate, bb, tt):
    b, t, d = x.shape
    tn = _tile(d, 1024, LANES)
    d1, d2, d3 = rw.shape[-1], mla.shape[-1], cv.shape[-1]
    return pl.pallas_call(
        _outproj_kernel,
        grid=(b // bb, t // tt, d // tn),
        in_specs=[pl.BlockSpec((bb, tt, d1), lambda i, j, k: (i, j, 0)),
                  pl.BlockSpec((bb, tt, d2), lambda i, j, k: (i, j, 0)),
                  pl.BlockSpec((bb, tt, d3), lambda i, j, k: (i, j, 0)),
                  pl.BlockSpec((1, d1 + d2 + d3, tn), lambda i, j, k: (layer, 0, k)),
                  pl.BlockSpec((bb, tt, tn), lambda i, j, k: (i, j, k)),
                  pl.BlockSpec((bb, 1, tn), lambda i, j, k: (i, 0, k))],
        out_specs=pl.BlockSpec((bb, tt, tn), lambda i, j, k: (i, j, k)),
        out_shape=jax.ShapeDtypeStruct((b, t, d), F32),
        compiler_params=_cp(("arbitrary", "arbitrary", "arbitrary")),
        name="outproj",
    )(rw, mla, cv, w_out_bf16, x, gate)


def _layout(d):
    rd = d // 4
    q_lora = d // 4
    kv_lora = 512
    mla_dim = MLA_HEADS * MLA_VHEAD
    cd = d - rd - mla_dim
    shift_dim = 3 * rd + 2 * RW_LORA
    src, pos = {}, 0
    for name, w in (("rw_pre", shift_dim), ("rw_gate", rd), ("cq", q_lora), ("ckv", kv_lora),
                    ("kr", MLA_ROPE), ("mla_gate", mla_dim), ("cv_b", cd), ("cv_c", cd),
                    ("cv_x", cd), ("cv_gate", cd)):
        src[name] = (pos, w)
        pos += w
    p0 = src["rw_pre"][0]
    src["r"], src["k"], src["v"] = (p0, rd), (p0 + rd, rd), (p0 + 2 * rd, rd)
    src["la"] = (p0 + 3 * rd, 2 * RW_LORA)
    head = ("r", "k", "v", "rw_gate", "cq", "ckv", "la")
    tail = ("mla_gate", "cv_b", "cv_c", "cv_x", "cv_gate")
    assert all(src[n][0] % LANES == 0 and src[n][1] % LANES == 0 for n in head)
    assert src["kr"][0] % LANES == 0 and src["kr"][1] == LANES // 2
    assert src[tail[0]][0] == src["kr"][0] + src["kr"][1]

    offs, src_tbl, mode_tbl, pos = {}, [], [], 0
    for name in head:
        offs[name] = pos
        for i in range(src[name][1] // LANES):
            src_tbl.append(src[name][0] // LANES + i)
            mode_tbl.append(WP_COPY)
        pos += src[name][1]
    offs["kr"] = pos
    kr_tile = src["kr"][0] // LANES
    src_tbl.append(kr_tile)
    mode_tbl.append(WP_HALF)
    pos += LANES
    while pos % 1024:
        src_tbl.append(kr_tile)
        mode_tbl.append(WP_ZERO)
        pos += LANES
    n_tail = 0
    for name in tail:
        offs[name] = pos
        for i in range(src[name][1] // LANES):
            n_tail += 1
            src_tbl.append(kr_tile + n_tail)
            mode_tbl.append(WP_SHIFT)
        pos += src[name][1]
    while pos % 1024:
        src_tbl.append(src_tbl[-1])
        mode_tbl.append(WP_ZERO)
        pos += LANES
    return src, offs, (tuple(src_tbl), tuple(mode_tbl)), pos, dict(
        rd=rd, q_lora=q_lora, kv_lora=kv_lora, cd=cd, shift_dim=shift_dim,
        in_cols=src["cv_gate"][0] + cd)


def _prep_layer(l, dims, w):
    rd = dims["rd"]
    w_uq =w["mla_w_uq"][l].reshape(-1, MLA_HEADS, MLA_NOPE + MLA_ROPE)
    w_uq_p = jnp.concatenate([w_uq[:, :, :MLA_NOPE].reshape(w_uq.shape[0], -1),
                              w_uq[:, :, MLA_NOPE:].reshape(w_uq.shape[0], -1)], axis=1).astype(BF16)
    zl = jnp.zeros((RW_LORA, rd), F32)
    w2a2 = jnp.concatenate([jnp.concatenate([w["rw_w2"][l], zl], axis=1),
                            jnp.concatenate([zl, w["rw_a2"][l]], axis=1)], axis=0).astype(BF16)
    mu = w["rw_mu"][l]
    scale = float(MLA_NOPE + MLA_ROPE) ** -0.5 * math.log2(math.e)
    idx = jnp.arange(LANES)
    bd = ((idx[:, None] // RW_HEAD) == (idx[None, :] // RW_HEAD)).astype(BF16)
    row = lambda v: v.reshape(1, -1)
    return dict(
        w_in=w["w_in_bf16"], norm_g=w["norm_g"][l],
        mu_r=row(mu[:rd]), mu_k=row(mu[rd:2 * rd]), mu_v=row(mu[2 * rd:3 * rd]), mu_la=row(mu[3 * rd:]),
        w2a2=w2a2, w0=row(w["rw_w0"][l]), a0=row(w["rw_a0"][l]), kk_g=row(w["rw_kk"][l]),
        ka_g=row(w["rw_ka"][l]), rk=row(w["rw_rk"][l]), ln_g=row(w["rw_ln_g"][l]),
        ln_b=row(w["rw_ln_b"][l]), bd=bd,
        qnorm_g=row(w["mla_qnorm_g"][l]), kvnorm_g=row(w["mla_kvnorm_g"][l]),
        w_uq=w_uq_p, w_uk=w["mla_w_uk"][l].astype(BF16), w_uv_t=w["mla_w_uv"][l].T.astype(BF16),
        qn_nope_g=row(jnp.tile(w["mla_qn_nope"][l], MLA_HEADS)) * scale,
        qn_rope_g=row(jnp.tile(w["mla_qn_rope"][l], MLA_HEADS)) * scale,
        kn_nope_g=row(jnp.tile(w["mla_kn_nope"][l], MLA_HEADS)),
        kn_rope_g=row(jnp.concatenate([w["mla_kn_rope"][l], jnp.zeros((LANES - MLA_ROPE,), F32)])),
        conv_w=w["conv_w"][l], conv_b=row(w["conv_b"][l]), w_out=w["w_out_bf16"], layer=l,
    )


def _rope_tables(past, t):
    half = MLA_ROPE // 2
    freqs = ROPE_BASE ** (-jnp.arange(half, dtype=F32) / half)
    ang = jnp.arange(past, past + t, dtype=jnp.int32).astype(F32)[:, None] * freqs[None, :]
    cos, sin = jnp.cos(ang), jnp.sin(ang)
    reps = LANES // MLA_ROPE
    return (jnp.concatenate([cos, cos] * reps, axis=1), jnp.concatenate([-sin, sin] * reps, axis=1))


def _pad_rows(x, n):
    if x.shape[1] == n:
        return x
    pad = [(0, 0)] * x.ndim
    pad[1] = (0, n - x.shape[1])
    return jnp.pad(x, pad)


def _mixer(x, t_real, past, mod, rw_state, rw_shift, conv_state, lat_past, kr_past,
           lp, offs, dims):
    b, t, d = x.shape
    rd = dims["rd"]
    shift, scale, gate = mod
    tt = _tile(t, ROW_TILE)
    bb = _tile(b, max(1, ROW_TILE // tt), 1) if tt == t else 1

    proj = _proj(x, lp["norm_g"], scale, shift, lp["w_in"], lp["layer"], bb, tt)

    n_pairs = rd // LANES
    if rw_state is None:
        s0_blk = jnp.zeros((b, n_pairs, LANES, LANES), F32)
        shift0 = dict(r=jnp.zeros((b, 1, rd), F32), k=jnp.zeros((b, 1, rd), F32),
                      v=jnp.zeros((b, 1, rd), F32), la=jnp.zeros((b, 1, LANES), F32))
    else:
        s4 = rw_state.astype(F32).reshape(b, n_pairs, 2, RW_HEAD, RW_HEAD)
        z = jnp.zeros_like(s4[:, :, 0])
        s0_blk = jnp.concatenate([jnp.concatenate([s4[:, :, 0], z], axis=-1),
                                  jnp.concatenate([z, s4[:, :, 1]], axis=-1)], axis=-2)
        sh = rw_shift[:, None, :]
        shift0 = dict(r=sh[..., :rd], k=sh[..., rd:2 * rd], v=sh[..., 2 * rd:3 * rd],
                      la=sh[..., 3 * rd:])
    rw_out, s_blk = _rwkv(proj, offs, shift0, s0_blk, lp, t_real)
    new_state = jnp.stack([s_blk[:, :, :RW_HEAD, :RW_HEAD], s_blk[:, :, RW_HEAD:, RW_HEAD:]],
                          axis=2).reshape(b, 2 * n_pairs, RW_HEAD, RW_HEAD)
    new_shift = jnp.concatenate([proj[:, t_real - 1, offs["r"]:offs["r"] + 3 * rd],
                                 proj[:, t_real - 1, offs["la"]:offs["la"] + 2 * RW_LORA]], axis=-1)

    cos_t, sin_t = _rope_tables(past, t)
    tt_s = _tile(t, 512)
    lat_new, kr_new = _latkr(proj, offs, lp, cos_t, sin_t, tt_s)
    q_cat = _qproj(proj, offs, lp, cos_t, sin_t, tt_s)
    kv_len = past + t_real
    if past == 0 and t_real == t:
        lat_all, kr_all, tk_pad = lat_new, kr_new, t
    else:
        tk_pad = -(-kv_len // LANES) * LANES
        lat_all = _pad_rows(jnp.concatenate([lat_past.astype(BF16),
                                             lat_new[:, :t_real].astype(BF16)], axis=1), tk_pad)
        kr_all = _pad_rows(jnp.concatenate([kr_past, kr_new[:, :t_real]], axis=1), tk_pad)
    krd = jnp.concatenate([kr_all, kr_all], axis=-1).astype(BF16)
    tt_kv = _tile(tk_pad, 512, LANES)
    if tt_kv >= 256:
        k_cat, v_t = _kv(lat_all, krd, lp, tt_kv, MLA_HEADS)
    else:
        k_cat, v_t = _kv(lat_all, krd, lp, tk_pad, 4)
    tq = _tile(t, 512)
    tk = tk_pad if t < 256 else _tile(tk_pad, 512, LANES)
    mla_out = _attn(q_cat, k_cat, v_t, proj, offs, past, kv_len, tq, tk)

    if conv_state is None:
        conv_state = jnp.zeros((b, CONV_W - 1, dims["cd"]), F32)
    cv_out, new_conv = _conv(proj, offs, conv_state, lp, tt_s, t_real)

    y = _outproj(rw_out, mla_out, cv_out, lp["w_out"], lp["layer"], x, gate, bb, tt)
    return (y, lat_new[:, :t_real], kr_new[:, :t_real], new_state, new_shift, new_conv)


def kernel(x_prompt, x_sample, c_prompt, c_sample, cache_mla_latent, cache_mla_krope, state_rwkv, state_rwkv_shift, state_conv, w_ada, b_ada, norm_g, w_in, rw_mu, rw_w0, rw_w2, rw_a0, rw_a2, rw_kk, rw_ka, rw_rk, rw_ln_g, rw_ln_b, mla_qnorm_g, mla_kvnorm_g, mla_w_uq, mla_w_uk, mla_w_uv, mla_qn_nope, mla_qn_rope, mla_kn_nope, mla_kn_rope, conv_w, conv_b, w_out):
    w = dict(norm_g=norm_g, rw_mu=rw_mu, rw_w0=rw_w0, rw_w2=rw_w2, rw_a0=rw_a0,
             rw_a2=rw_a2, rw_kk=rw_kk, rw_ka=rw_ka, rw_rk=rw_rk, rw_ln_g=rw_ln_g, rw_ln_b=rw_ln_b,
             mla_qnorm_g=mla_qnorm_g, mla_kvnorm_g=mla_kvnorm_g, mla_w_uq=mla_w_uq,
             mla_w_uk=mla_w_uk, mla_w_uv=mla_w_uv, mla_qn_nope=mla_qn_nope, mla_qn_rope=mla_qn_rope,
             mla_kn_nope=mla_kn_nope, mla_kn_rope=mla_kn_rope, conv_w=conv_w, conv_b=conv_b,
             w_out_bf16=w_out.astype(BF16))
    depth = w_in.shape[0]
    bp, tp, d = x_prompt.shape
    bs, ts, _ = x_sample.shape
    past = cache_mla_latent.shape[2]
    src, offs, tables, total, dims = _layout(d)
    assert dims["in_cols"] == w_in.shape[2] and tp % RW_L == 0
    w["w_in_bf16"] = _winprep(w_in, tables)

    rows = -(-(bp + bs) // 16) * 16
    c_all = jnp.concatenate([c_prompt, c_sample, jnp.zeros((rows - bp - bs, d), F32)], axis=0)
    mod = _ada(c_all.astype(BF16), w_ada, b_ada)

    def mods(l, lo, n):
        m = mod[l, lo:lo + n][:, None, :]
        return m[..., :d], m[..., d:2 * d], m[..., 2 * d:]

    ts_pad = -(-ts // RW_L) * RW_L
    yp, ys = x_prompt, _pad_rows(x_sample, ts_pad)
    outs_p, outs_s = [], []
    for l in range(depth):
        lp = _prep_layer(l, dims, w)
        rp = _mixer(yp, tp, 0, mods(l, 0, bp), None, None, None, None, None, lp, offs, dims)
        yp = rp[0]
        outs_p.append(rp[1:])
        rs = _mixer(ys, ts, past, mods(l, bp, bs), state_rwkv[l], state_rwkv_shift[l],
                    state_conv[l], cache_mla_latent[l], cache_mla_krope[l], lp, offs, dims)
        ys = rs[0]
        outs_s.append(rs[1:])

    def stack(outs, i):
        return jnp.stack([o[i] for o in outs])

    return (yp, ys[:, :ts],
            stack(outs_p, 0), stack(outs_p, 1), stack(outs_p, 2), stack(outs_p, 3), stack(outs_p, 4),
            stack(outs_s, 0), stack(outs_s, 1), stack(outs_s, 2), stack(outs_s, 3), stack(outs_s, 4))
```
